```python
import math, functools
import jax, jax.numpy as jnp
from jax import lax
import numpy as np

D_MODEL = 1024
BATCH = 8
SEQ = 4096
DEPTH = 1
DEC_BATCH = 128
DEC_SEQ = 4
PAST_LEN = 16384
PAGE_SIZE = 128

A_HEADS = 8
A_KDIM = 128
A_VDIM = D_MODEL // A_HEADS
A_KW = A_HEADS * A_KDIM
A_VW = A_HEADS * A_VDIM
A_CHUNK = 64
B_QHEADS = 16
B_KVHEADS = 2
B_HDIM = 64
B_GROUP = B_QHEADS // B_KVHEADS
B_QW = B_QHEADS * B_HDIM
B_KVW = B_KVHEADS * B_HDIM
WINDOW = 128
EPS = 1e-6
NEG = -1e30
SPLITS = (A_KW, A_KW, A_VW, A_VW, B_QW, B_KVW, B_KVW, B_QW, D_MODEL, D_MODEL)
IN_COLS = A_KW + A_KW + A_VW + A_VW + B_QW + B_KVW + B_KVW + B_QW + D_MODEL + D_MODEL

kernel_name = "hgrn2_swa_sink_gated_hybrid_step"


def rmsnorm(x, g):
    xf = x.astype(jnp.float32)
    y = xf * lax.rsqrt(jnp.mean(xf * xf, axis=-1, keepdims=True) + EPS)
    return (y * g.astype(jnp.float32)).astype(x.dtype)


def alibi_slopes(n):
    return 2.0 ** (-8.0 * jnp.arange(1, n + 1, dtype=jnp.float32) / n)


def in_projection(xn, w_in):
    h = jnp.einsum('btd,dc->btc', xn, w_in)
    offs = [int(o) for o in np.cumsum(SPLITS)[:-1]]
    return jnp.split(h, offs, axis=-1)


def hgrn2_features(qa, fa, ia, lb):
    B, T, _ = qa.shape
    q = jax.nn.silu(qa.astype(jnp.float32))
    sig = jax.nn.sigmoid(fa.astype(jnp.float32))
    lbf = lb.astype(jnp.float32)
    f = lbf + (1.0 - lbf) * sig
    k = (1.0 - lbf) * (1.0 - sig)
    logf = jnp.log(f)
    shp = (B, T, A_HEADS, A_KDIM)
    v = ia.astype(jnp.float32).reshape(B, T, A_HEADS, A_VDIM)
    return q.reshape(shp), k.reshape(shp), v, logf.reshape(shp)


def hgrn2_chunk(S, q, k, v, logf):
    C = q.shape[1]
    b = jnp.cumsum(logf, axis=1)
    causal = jnp.tril(jnp.ones((C, C), dtype=bool))[None, :, :, None, None]
    diff = b[:, :, None] - b[:, None, :]
    decay = jnp.where(causal, jnp.exp(jnp.where(causal, diff, 0.0)), 0.0)
    att = jnp.einsum('bthk,bshk,btshk->bhts', q, k, decay)
    o = jnp.einsum('bhts,bshv->bthv', att, v) + jnp.einsum('bthk,bhkv->bthv', q * jnp.exp(b), S)
    b_last = b[:, -1]
    k_dec = k * jnp.exp(b_last[:, None] - b)
    S_new = jnp.exp(b_last)[..., None] * S + jnp.einsum('bshk,bshv->bhkv', k_dec, v)
    return S_new, o


def hgrn2_prompt(q, k, v, logf):
    B, T = q.shape[:2]
    n_chunks = T // A_CHUNK

    def to_chunks(a):
        return a.reshape((B, n_chunks, A_CHUNK) + a.shape[2:]).swapaxes(0, 1)

    S0 = jnp.zeros((B, A_HEADS, A_KDIM, A_VDIM), jnp.float32)
    S_fin, o = lax.scan(lambda S, inp: hgrn2_chunk(S, *inp), S0,
                        (to_chunks(q), to_chunks(k), to_chunks(v), to_chunks(logf)))
    o = o.swapaxes(0, 1).reshape(B, T, A_HEADS, A_VDIM)
    return o, S_fin


def hgrn2_sample(q, k, v, logf, S):
    S_new, o = hgrn2_chunk(S.astype(jnp.float32), q, k, v, logf)
    return o, S_new


def sink_softmax(s, sinks):
    sk = jnp.broadcast_to(sinks.astype(jnp.float32).reshape(B_KVHEADS, B_GROUP, 1, 1), s.shape[:-1] + (1,))
    return jax.nn.softmax(jnp.concatenate([s, sk], axis=-1), axis=-1)[..., :-1]


def swa_prompt(q, k, v, sinks):
    B, T, _ = q.shape
    nb = T // WINDOW
    scale = 1.0 / math.sqrt(B_HDIM)
    qb = q.reshape(B, nb, WINDOW, B_KVHEADS, B_GROUP, B_HDIM)
    kb = k.reshape(B, nb, WINDOW, B_KVHEADS, B_HDIM)
    vb = v.reshape(B, nb, WINDOW, B_KVHEADS, B_HDIM)

    def prev(a):
        return jnp.concatenate([jnp.zeros_like(a[:, :1]), a[:, :-1]], axis=1)

    kk = jnp.concatenate([prev(kb), kb], axis=2)
    vv = jnp.concatenate([prev(vb), vb], axis=2)
    qi = jnp.arange(WINDOW)[:, None]
    kj = jnp.arange(2 * WINDOW)[None, :] - WINDOW
    dist = (qi - kj).astype(jnp.float32)
    band = ((qi - kj) >= 0) & ((qi - kj) < WINDOW)
    valid = band[None] & ((jnp.arange(nb)[:, None, None] > 0) | (kj >= 0)[None])
    s = jnp.einsum('bnqhgd,bnshd->bnhgqs', qb, kk).astype(jnp.float32) * scale
    s = s - alibi_slopes(B_QHEADS).reshape(B_KVHEADS, B_GROUP, 1, 1) * dist
    s = jnp.where(valid[None, :, None, None], s, NEG)
    p = sink_softmax(s, sinks)
    o = jnp.einsum('bnhgqs,bnshd->bnqhgd', p, vv.astype(jnp.float32)).reshape(B, T, B_QW)
    L = min(WINDOW, T)
    ck = k[:, T - L:].reshape(B, L, B_KVHEADS, B_HDIM)
    cv = v[:, T - L:].reshape(B, L, B_KVHEADS, B_HDIM)
    return o, ck, cv


def swa_sample(q, k, v, sinks, ck, cv):
    DB, T, _ = q.shape
    L = ck.shape[1]
    scale = 1.0 / math.sqrt(B_HDIM)
    qr = q.reshape(DB, T, B_KVHEADS, B_GROUP, B_HDIM)
    kk = jnp.concatenate([ck.astype(k.dtype), k.reshape(DB, T, B_KVHEADS, B_HDIM)], axis=1)
    vv = jnp.concatenate([cv.astype(v.dtype), v.reshape(DB, T, B_KVHEADS, B_HDIM)], axis=1)
    qpos = jnp.arange(T)[:, None]
    kpos = jnp.concatenate([jnp.arange(L) - L, jnp.arange(T)])[None, :]
    d = qpos - kpos
    valid = (d >= 0) & (d < WINDOW)
    s = jnp.einsum('bqhgd,bshd->bhgqs', qr, kk).astype(jnp.float32) * scale
    s = s - alibi_slopes(B_QHEADS).reshape(B_KVHEADS, B_GROUP, 1, 1) * d.astype(jnp.float32)
    s = jnp.where(valid, s, NEG)
    p = sink_softmax(s, sinks)
    o = jnp.einsum('bhgqs,bshd->bqhgd', p, vv.astype(jnp.float32)).reshape(DB, T, B_QW)
    return o, kk[:, -L:], vv[:, -L:]


def decoder_layer(x, hgrn_fn, swa_fn, norm_in, w_in, lb, a_gnorm, b_sinks, w_pa, w_pb, w_out):
    xn = rmsnorm(x, norm_in)
    qa, fa, ia, za, qb, kb, vb, zb, ga, gb = in_projection(xn, w_in)
    q, k, v, logf = hgrn2_features(qa, fa, ia, lb)
    oa, S_new = hgrn_fn(q, k, v, logf)
    oa = oa * lax.rsqrt(jnp.mean(oa * oa, axis=-1, keepdims=True) + EPS) * a_gnorm.astype(jnp.float32)
    B, T = x.shape[:2]
    ha = oa.reshape(B, T, A_VW) * jax.nn.silu(za.astype(jnp.float32))
    ob, ck, cv = swa_fn(qb, kb, vb, b_sinks)
    hb = ob * jax.nn.silu(zb.astype(jnp.float32))
    pa = jnp.einsum('btc,cd->btd', ha.astype(x.dtype), w_pa)
    pb = jnp.einsum('btc,cd->btd', hb.astype(x.dtype), w_pb)
    merged = jax.nn.sigmoid(ga.astype(jnp.float32)) * pa + jax.nn.sigmoid(gb.astype(jnp.float32)) * pb
    y = x + jnp.einsum('btd,de->bte', merged.astype(x.dtype), w_out)
    return y.astype(x.dtype), S_new, ck, cv


def setup_inputs(seed: int = 0) -> dict:
    key = jax.random.key(seed)
    ks = jax.random.split(key, 16)
    L = min(WINDOW, PAST_LEN)
    f32 = jnp.float32
    return {
        "x_prompt": jax.random.normal(ks[0], (BATCH, SEQ, D_MODEL), f32),
        "x_sample": jax.random.normal(ks[1], (DEC_BATCH, DEC_SEQ, D_MODEL), f32),
        "state_hgrn": 0.5 * jax.random.normal(ks[2], (DEPTH, DEC_BATCH, A_HEADS, A_KDIM, A_VDIM), f32),
        "cache_k": jax.random.normal(ks[3], (DEPTH, DEC_BATCH, L, B_KVHEADS, B_HDIM), f32),
        "cache_v": jax.random.normal(ks[4], (DEPTH, DEC_BATCH, L, B_KVHEADS, B_HDIM), f32),
        "norm_in": 1.0 + 0.02 * jax.random.normal(ks[5], (DEPTH, D_MODEL), f32),
        "w_in": jax.random.normal(ks[6], (DEPTH, D_MODEL, IN_COLS), f32) * D_MODEL ** -0.5,
        "lb_logits": 0.5 * jax.random.normal(ks[7], (DEPTH + 1, A_KW), f32),
        "a_gnorm": 1.0 + 0.02 * jax.random.normal(ks[8], (DEPTH, A_VDIM), f32),
        "b_sinks": 0.5 * jax.random.normal(ks[9], (DEPTH, B_QHEADS), f32),
        "w_pa": jax.random.normal(ks[10], (DEPTH, A_VW, D_MODEL), f32) * A_VW ** -0.5,
        "w_pb": jax.random.normal(ks[11], (DEPTH, B_QW, D_MODEL), f32) * B_QW ** -0.5,
        "w_out": jax.random.normal(ks[12], (DEPTH, D_MODEL, D_MODEL), f32) * D_MODEL ** -0.5,
        "norm_final": 1.0 + 0.02 * jax.random.normal(ks[13], (D_MODEL,), f32),
    }


def reference(x_prompt, x_sample, state_hgrn, cache_k, cache_v, norm_in, w_in, lb_logits, a_gnorm,
              b_sinks, w_pa, w_pb, w_out, norm_final):
    lbs = jnp.cumsum(jax.nn.softmax(lb_logits.astype(jnp.float32), axis=0), axis=0)
    hp, hs = x_prompt, x_sample
    sp, kp, vp, ss, ksm, vsm = [], [], [], [], [], []
    for l in range(DEPTH):
        w = (norm_in[l], w_in[l], lbs[l], a_gnorm[l], b_sinks[l], w_pa[l], w_pb[l], w_out[l])
        hp, s_p, k_p, v_p = decoder_layer(hp, hgrn2_prompt, swa_prompt, *w)
        hs, s_s, k_s, v_s = decoder_layer(
            hs, functools.partial(hgrn2_sample, S=state_hgrn[l]),
            functools.partial(swa_sample, ck=cache_k[l], cv=cache_v[l]), *w)
        sp.append(s_p.astype(state_hgrn.dtype)); kp.append(k_p.astype(cache_k.dtype)); vp.append(v_p.astype(cache_v.dtype))
        ss.append(s_s.astype(state_hgrn.dtype)); ksm.append(k_s.astype(cache_k.dtype)); vsm.append(v_s.astype(cache_v.dtype))
    y_prompt = rmsnorm(hp, norm_final)
    y_sample = rmsnorm(hs, norm_final)
    state_hgrn_prompt = jnp.stack(sp)
    cache_k_prompt = jnp.stack(kp)
    cache_v_prompt = jnp.stack(vp)
    state_hgrn_sample = jnp.stack(ss)
    cache_k_sample = jnp.stack(ksm)
    cache_v_sample = jnp.stack(vsm)
    return (y_prompt, y_sample, state_hgrn_prompt, cache_k_prompt, cache_v_prompt,
            state_hgrn_sample, cache_k_sample, cache_v_sample)
```

```python
import functools
import math

import jax
import jax.numpy as jnp
from jax import lax
from jax.experimental import pallas as pl
from jax.experimental.pallas import tpu as pltpu

F32 = jnp.float32
BF16 = jnp.bfloat16

D_MODEL = 1024
A_HEADS = 8
A_KDIM = 128
A_VDIM = D_MODEL // A_HEADS
A_KW = A_HEADS * A_KDIM
A_VW = A_HEADS * A_VDIM
B_QHEADS = 16
B_KVHEADS = 2
B_HDIM = 64
B_GROUP = B_QHEADS // B_KVHEADS
B_QW = B_QHEADS * B_HDIM
B_KVW = B_KVHEADS * B_HDIM
WINDOW = 128
EPS = 1e-6
NEG = -1e30
SAMPLE_PAD = 16

_SPLITS = (A_KW, A_KW, A_VW, A_VW, B_QW, B_KVW, B_KVW, B_QW, D_MODEL, D_MODEL)
_OFFS = tuple(int(sum(_SPLITS[:i])) for i in range(len(_SPLITS)))
IN_COLS = int(sum(_SPLITS))

VMEM_LIMIT = 56 * 1024 * 1024

HGRN_CHUNK = 64
HGRN_SUB = 16


def _sigmoid(x):
    return 1.0 / (1.0 + jnp.exp(-x))


def _silu(x):
    return x * _sigmoid(x)


def _dot_nt(a, b):
    return lax.dot_general(a, b, (((1,), (1,)), ((), ())), preferred_element_type=F32)


def _dot_tn(a, b):
    return lax.dot_general(a, b, (((0,), (0,)), ((), ())), preferred_element_type=F32)


def _dot(a, b):
    return jnp.dot(a, b, preferred_element_type=F32)


def _inproj_kernel(x_ref, g_ref, lbl_ref, w_ref,
                   q_ref, k_ref, lf_ref, v_ref, za_ref, qb_ref, kv_ref, zb_ref, gg_ref):
    x = x_ref[...]
    ms = jnp.mean(x * x, axis=-1, keepdims=True)
    xn = (x * lax.rsqrt(ms + EPS) * g_ref[...]).astype(BF16)

    def seg(i, width=None):
        width = _SPLITS[i] if width is None else width
        return _dot(xn, w_ref[:, _OFFS[i]:_OFFS[i] + width])

    l = lbl_ref[...]
    e = jnp.exp(l - jnp.max(l, axis=0, keepdims=True))
    lb = e[0:1, :] / jnp.sum(e, axis=0, keepdims=True)

    q_ref[...] = _silu(seg(0)).astype(BF16)
    sig = _sigmoid(seg(1))
    k_ref[...] = ((1.0 - lb) * (1.0 - sig)).astype(BF16)
    lf_ref[...] = jnp.log(lb + (1.0 - lb) * sig)
    v_ref[...] = seg(2).astype(BF16)
    za_ref[...] = _silu(seg(3)).astype(BF16)
    qb_ref[...] = (seg(4) * (1.0 / math.sqrt(B_HDIM))).astype(BF16)
    kv_ref[...] = seg(5, 2 * B_KVW)
    zb_ref[...] = _silu(seg(7)).astype(BF16)
    gg_ref[...] = _sigmoid(seg(8, 2 * D_MODEL)).astype(BF16)


def _in_projection(x2d, norm_in, lb_logits, w_in_bf16, tm):
    m = x2d.shape[0]
    assert m % tm == 0
    row = lambda w: pl.BlockSpec((tm, w), lambda i: (i, 0))
    const = lambda shape: pl.BlockSpec(shape, lambda i: (0, 0), pipeline_mode=pl.Buffered(1))
    out_shapes = (
        jax.ShapeDtypeStruct((m, A_KW), BF16),
        jax.ShapeDtypeStruct((m, A_KW), BF16),
        jax.ShapeDtypeStruct((m, A_KW), F32),
        jax.ShapeDtypeStruct((m, A_VW), BF16),
        jax.ShapeDtypeStruct((m, A_VW), BF16),
        jax.ShapeDtypeStruct((m, B_QW), BF16),
        jax.ShapeDtypeStruct((m, 2 * B_KVW), F32),
        jax.ShapeDtypeStruct((m, B_QW), BF16),
        jax.ShapeDtypeStruct((m, 2 * D_MODEL), BF16),
    )
    return pl.pallas_call(
        _inproj_kernel,
        grid=(m // tm,),
        in_specs=[row(D_MODEL), const((1, D_MODEL)), const(lb_logits.shape), const((D_MODEL, IN_COLS))],
        out_specs=tuple(row(s.shape[1]) for s in out_shapes),
        out_shape=out_shapes,
        compiler_params=pltpu.CompilerParams(dimension_semantics=("arbitrary",), vmem_limit_bytes=VMEM_LIMIT),
        name="in_projection",
    )(x2d, norm_in.reshape(1, D_MODEL), lb_logits, w_in_bf16)


def _block_cumsum(x, sub):
    assert sub & (sub - 1) == 0
    rows = lax.broadcasted_iota(jnp.int32, x.shape, 0) & (sub - 1)
    shift = 1
    while shift < sub:
        x = x + jnp.where(rows >= shift, pltpu.roll(x, shift, axis=0), 0.0)
        shift *= 2
    return x


def _rows(vecs, sub):
    return jnp.concatenate([jnp.broadcast_to(v, (sub, v.shape[1])) for v in vecs], axis=0)


def _hgrn_chunk(q, k, v, lf, st, chunk, sub):
    n = chunk // sub
    qf = q.astype(F32)
    kf = k.astype(F32)
    ell = _block_cumsum(lf, sub)
    tau = [ell[(i + 1) * sub - 1:(i + 1) * sub, :] for i in range(n)]
    mu = [ell[i * sub + sub // 2 - 1:i * sub + sub // 2, :] for i in range(n)]
    beta = [jnp.zeros_like(tau[0])]
    for i in range(n):
        beta.append(beta[i] + tau[i])
    total = beta[n]

    qt = qf * jnp.exp(ell)
    kt = kf * jnp.exp(_rows(tau, sub) - ell)
    mu_rows = _rows(mu, sub)
    qd = (qf * jnp.exp(ell - mu_rows)).astype(BF16)
    kd = (kf * jnp.exp(mu_rows - ell)).astype(BF16)
    qhat = (qt * _rows([jnp.exp(b) for b in beta[:n]], sub)).astype(BF16)
    khat = (kt * _rows([jnp.exp(total - b) for b in beta[1:]], sub)).astype(BF16)

    o = _dot_nt(qhat, st.astype(BF16))

    r = lax.broadcasted_iota(jnp.int32, (chunk, chunk), 0)
    s = lax.broadcasted_iota(jnp.int32, (chunk, chunk), 1)
    att = jnp.where(s <= r, _dot_nt(qd, kd), 0.0)
    if n > 1:
        zero_blk = jnp.zeros((sub, A_KDIM), F32)
        off = [jnp.zeros((sub, chunk), F32)]
        for i in range(1, n):
            kin = [kt[j * sub:(j + 1) * sub] * jnp.exp(beta[i] - beta[j + 1]) for j in range(i)]
            kin = jnp.concatenate(kin + [zero_blk] * (n - i), axis=0).astype(BF16)
            off.append(_dot_nt(qt[i * sub:(i + 1) * sub].astype(BF16), kin))
        same = ((r ^ s) & ~(sub - 1)) == 0
        att = jnp.where(same, att, jnp.concatenate(off, axis=0))
    o = o + _dot(att.astype(BF16), v)
    st_new = st * jnp.exp(total) + _dot_tn(v, khat)
    return o, st_new


def _hgrn_kernel(*refs, chunk, sub, t_valid, has_state):
    if has_state:
        q_ref, k_ref, v_ref, lf_ref, z_ref, gn_ref, s0_ref, o_ref, sout_ref, st_ref = refs
    else:
        q_ref, k_ref, v_ref, lf_ref, z_ref, gn_ref, o_ref, sout_ref, st_ref = refs
        s0_ref = None
    tb = q_ref.shape[1]
    nt = pl.program_id(1)

    @pl.when(nt == 0)
    def _():
        for h in range(A_HEADS):
            st_ref[h] = s0_ref[0, h].T if has_state else jnp.zeros((A_VDIM, A_KDIM), F32)

    gn = gn_ref[...]

    def body(ci, carry):
        r0 = pl.multiple_of(ci * chunk, chunk)
        rs = pl.ds(r0, chunk)
        for h in range(A_HEADS):
            cs = slice(h * A_KDIM, (h + 1) * A_KDIM)
            lf = lf_ref[0, rs, cs]
            if t_valid is not None:
                rows = lax.broadcasted_iota(jnp.int32, lf.shape, 0)
                lf = jnp.where(rows < t_valid, lf, 0.0)
            o, st_new = _hgrn_chunk(q_ref[0, rs, cs], k_ref[0, rs, cs], v_ref[0, rs, cs], lf, st_ref[h], chunk, sub)
            st_ref[h] = st_new
            o = o * lax.rsqrt(jnp.mean(o * o, axis=-1, keepdims=True) + EPS) * gn
            o_ref[0, rs, cs] = (o * z_ref[0, rs, cs].astype(F32)).astype(o_ref.dtype)
        return carry

    lax.fori_loop(0, tb // chunk, body, 0)

    @pl.when(nt == pl.num_programs(1) - 1)
    def _():
        for h in range(A_HEADS):
            sout_ref[0, h] = st_ref[h].T


def _hgrn(q, k, v, lf, z, a_gnorm, state, *, tb, chunk, sub, t_valid=None):
    b, t, _ = q.shape
    assert t % tb == 0 and tb % chunk == 0 and chunk % sub == 0
    has_state = state is not None
    blk = lambda w: pl.BlockSpec((1, tb, w), lambda i, j: (i, j, 0))
    sblk = pl.BlockSpec((1, A_HEADS, A_KDIM, A_VDIM), lambda i, j: (i, 0, 0, 0))
    in_specs = [blk(A_KW), blk(A_KW), blk(A_VW), blk(A_KW), blk(A_VW),
                pl.BlockSpec((1, A_VDIM), lambda i, j: (0, 0))]
    args = [q, k, v, lf, z, a_gnorm.reshape(1, A_VDIM)]
    if has_state:
        in_specs.append(sblk)
        args.append(state)
    return pl.pallas_call(
        functools.partial(_hgrn_kernel, chunk=chunk, sub=sub, t_valid=t_valid, has_state=has_state),
        grid=(b, t // tb),
        in_specs=in_specs,
        out_specs=(blk(A_VW), sblk),
        out_shape=(jax.ShapeDtypeStruct((b, t, A_VW), BF16),
                   jax.ShapeDtypeStruct((b, A_HEADS, A_KDIM, A_VDIM), F32)),
        scratch_shapes=[pltpu.VMEM((A_HEADS, A_VDIM, A_KDIM), F32)],
        compiler_params=pltpu.CompilerParams(dimension_semantics=("arbitrary", "arbitrary"),
                                             vmem_limit_bytes=VMEM_LIMIT),
        name="hgrn2_sample" if has_state else "hgrn2_prompt",
    )(*args)


def _alibi_slope(h):
    return 2.0 ** (-8.0 * (h + 1) / B_QHEADS)


def _lane_halves(x):
    lane = lax.broadcasted_iota(jnp.int32, x.shape, 1)
    lo = jnp.where(lane < B_HDIM, x, 0.0)
    hi = jnp.where(lane >= B_HDIM, x, 0.0)
    lo_r = pltpu.roll(lo, B_HDIM, axis=1)
    hi_r = pltpu.roll(hi, B_HDIM, axis=1)
    return ((lo, lo_r), (hi_r, hi))


def _swa_heads(q_slab_fn, key_sets, sinks_ref, out_fn):
    for p in range(B_QHEADS // 2):
        q = q_slab_fn(p)
        rows = q.shape[0]
        lane = lax.broadcasted_iota(jnp.int32, (rows, 2 * B_HDIM), 1)
        acc = None
        inv = []
        for par in range(2):
            h = 2 * p + par
            kvh = h // B_GROUP
            sc = [_dot_nt(q, ks[kvh][par]) + bias(h) for ks, _, bias in key_sets]
            sink = sinks_ref[h]
            m = jnp.maximum(functools.reduce(jnp.maximum, [jnp.max(s, axis=-1, keepdims=True) for s in sc]), sink)
            ps = [jnp.exp(s - m) for s in sc]
            den = functools.reduce(lambda a, b: a + b, [jnp.sum(pp, axis=-1, keepdims=True) for pp in ps])
            inv.append(1.0 / (den + jnp.exp(sink - m)))
            for pp, (_, vs, _) in zip(ps, key_sets):
                part = _dot(pp.astype(BF16), vs[kvh][par])
                acc = part if acc is None else acc + part
        out_fn(p, acc * jnp.where(lane < B_HDIM, inv[0], inv[1]))


def _swa_prompt_kernel(sinks_ref, q_ref, kvp_ref, kvc_ref, z_ref, o_ref, bias_ref):
    nb = pl.program_id(1)
    w = WINDOW

    @pl.when((pl.program_id(0) == 0) & (nb == 0))
    def _():
        qi = lax.broadcasted_iota(jnp.int32, (w, 2 * w), 0)
        kj = lax.broadcasted_iota(jnp.int32, (w, 2 * w), 1) - w
        dist = qi - kj
        band = (dist >= 0) & (dist < w)
        for h in range(B_QHEADS):
            al = -_alibi_slope(h) * dist.astype(F32)
            bias_ref[0, h] = jnp.where(band & (kj >= 0), al, NEG)
            bias_ref[1, h] = jnp.where(band, al, NEG)

    sel = jnp.minimum(nb, 1)
    kv = jnp.concatenate([kvp_ref[0], kvc_ref[0]], axis=0)
    kh = _lane_halves(kv[:, :B_KVW])
    vh = _lane_halves(kv[:, B_KVW:])
    kh = [[x.astype(BF16) for x in pair] for pair in kh]
    vh = [[x.astype(BF16) for x in pair] for pair in vh]

    def out_fn(p, slab):
        cs = slice(p * 128, (p + 1) * 128)
        o_ref[0, :, cs] = (slab * z_ref[0, :, cs].astype(F32)).astype(o_ref.dtype)

    _swa_heads(lambda p: q_ref[0, :, p * 128:(p + 1) * 128],
               [(kh, vh, lambda h: bias_ref[sel, h])], sinks_ref, out_fn)


def _swa_prompt(qb, kv, zb, sinks):
    b, t, _ = qb.shape
    nb = t // WINDOW
    return pl.pallas_call(
        _swa_prompt_kernel,
        grid=(b, nb),
        in_specs=[pl.BlockSpec(memory_space=pltpu.SMEM),
                  pl.BlockSpec((1, WINDOW, B_QW), lambda i, j: (i, j, 0)),
                  pl.BlockSpec((1, WINDOW, 2 * B_KVW), lambda i, j: (i, jnp.maximum(j - 1, 0), 0)),
                  pl.BlockSpec((1, WINDOW, 2 * B_KVW), lambda i, j: (i, j, 0)),
                  pl.BlockSpec((1, WINDOW, B_QW), lambda i, j: (i, j, 0))],
        out_specs=pl.BlockSpec((1, WINDOW, B_QW), lambda i, j: (i, j, 0)),
        out_shape=jax.ShapeDtypeStruct((b, t, B_QW), BF16),
        scratch_shapes=[pltpu.VMEM((2, B_QHEADS, WINDOW, 2 * WINDOW), F32)],
        compiler_params=pltpu.CompilerParams(dimension_semantics=("arbitrary", "arbitrary"),
                                             vmem_limit_bytes=VMEM_LIMIT),
        name="swa_prompt",
    )(sinks, qb, kv, kv, zb)


def _swa_sample_kernel(sinks_ref, q_ref, kvn_ref, ck_ref, cv_ref, z_ref, o_ref, cko_ref, cvo_ref, *, t_new, seqs):
    w = ck_ref.shape[1]
    tp = q_ref.shape[1]
    t_c = lax.broadcasted_iota(jnp.int32, (tp, w), 0)
    i_c = lax.broadcasted_iota(jnp.int32, (tp, w), 1)
    d_c = t_c - (i_c - w)
    ok_c = (d_c >= 0) & (d_c < WINDOW)
    t_n = lax.broadcasted_iota(jnp.int32, (tp, tp), 0)
    s_n = lax.broadcasted_iota(jnp.int32, (tp, tp), 1)
    d_n = t_n - s_n
    ok_n = (d_n >= 0) & (d_n < WINDOW) & (s_n < t_new)

    def bias_c(h):
        return jnp.where(ok_c, -_alibi_slope(h) * d_c.astype(F32), NEG)

    def bias_n(h):
        return jnp.where(ok_n, -_alibi_slope(h) * d_n.astype(F32), NEG)

    for b in range(seqs):
        ck = ck_ref[b]
        cv = cv_ref[b]
        kvn = kvn_ref[b]
        kn = kvn[:, :B_KVW]
        vn = kvn[:, B_KVW:]
        bf = lambda halves: [[x.astype(BF16) for x in pair] for pair in halves]
        sets = [(bf(_lane_halves(ck)), bf(_lane_halves(cv)), bias_c),
                (bf(_lane_halves(kn)), bf(_lane_halves(vn)), bias_n)]

        def out_fn(p, slab, b=b):
            cs = slice(p * 128, (p + 1) * 128)
            o_ref[b, :, cs] = (slab * z_ref[b, :, cs].astype(F32)).astype(o_ref.dtype)

        _swa_heads(lambda p, b=b: q_ref[b, :, p * 128:(p + 1) * 128], sets, sinks_ref, out_fn)

        cko_ref[b, 0:w - t_new, :] = ck[t_new:w, :]
        cko_ref[b, w - t_new:w, :] = kn[0:t_new, :]
        cvo_ref[b, 0:w - t_new, :] = cv[t_new:w, :]
        cvo_ref[b, w - t_new:w, :] = vn[0:t_new, :]


def _swa_sample(qb, kvn, zb, sinks, cache_k, cache_v, *, t_new, seqs):
    b, tp, _ = qb.shape
    w = cache_k.shape[1]
    assert b % seqs == 0
    blk = lambda r, c: pl.BlockSpec((seqs, r, c), lambda i: (i, 0, 0))
    return pl.pallas_call(
        functools.partial(_swa_sample_kernel, t_new=t_new, seqs=seqs),
        grid=(b // seqs,),
        in_specs=[pl.BlockSpec(memory_space=pltpu.SMEM),
                  blk(tp, B_QW), blk(tp, 2 * B_KVW), blk(w, B_KVW), blk(w, B_KVW), blk(tp, B_QW)],
        out_specs=(blk(tp, B_QW), blk(w, B_KVW), blk(w, B_KVW)),
        out_shape=(jax.ShapeDtypeStruct((b, tp, B_QW), BF16),
                   jax.ShapeDtypeStruct((b, w, B_KVW), F32),
                   jax.ShapeDtypeStruct((b, w, B_KVW), F32)),
        compiler_params=pltpu.CompilerParams(dimension_semantics=("arbitrary",), vmem_limit_bytes=VMEM_LIMIT),
        name="swa_sample",
    )(sinks, qb, kvn, cache_k, cache_v, zb)


def _outproj_kernel(x_ref, ha_ref, hb_ref, gg_ref, wpa_ref, wpb_ref, wo_ref, gf_ref, y_ref):
    pa = _dot(ha_ref[...], wpa_ref[...])
    pb = _dot(hb_ref[...], wpb_ref[...])
    merged = gg_ref[:, :D_MODEL].astype(F32) * pa + gg_ref[:, D_MODEL:].astype(F32) * pb
    y = x_ref[...] + _dot(merged.astype(BF16), wo_ref[...])
    ms = jnp.mean(y * y, axis=-1, keepdims=True)
    y_ref[...] = y * lax.rsqrt(ms + EPS) * gf_ref[...]


def _out_projection(x2d, ha, hb, gg, w_pa, w_pb, w_out, norm_final, tm):
    m = x2d.shape[0]
    assert m % tm == 0
    row = lambda w: pl.BlockSpec((tm, w), lambda i: (i, 0))
    const = lambda shape: pl.BlockSpec(shape, lambda i: (0, 0), pipeline_mode=pl.Buffered(1))
    return pl.pallas_call(
        _outproj_kernel,
        grid=(m // tm,),
        in_specs=[row(D_MODEL), row(A_VW), row(B_QW), row(2 * D_MODEL),
                  const((A_VW, D_MODEL)), const((B_QW, D_MODEL)), const((D_MODEL, D_MODEL)), const((1, D_MODEL))],
        out_specs=row(D_MODEL),
        out_shape=jax.ShapeDtypeStruct((m, D_MODEL), F32),
        compiler_params=pltpu.CompilerParams(dimension_semantics=("arbitrary",), vmem_limit_bytes=VMEM_LIMIT),
        name="out_projection",
    )(x2d, ha, hb, gg, w_pa, w_pb, w_out, norm_final.reshape(1, D_MODEL))


def _layer(x, weights, *, hgrn_state, cache, tm, hgrn_tb, hgrn_chunk, hgrn_sub, t_valid):
    norm_in, w_in, lb_logits, a_gnorm, b_sinks, w_pa, w_pb, w_out, norm_final = weights
    b, t, d = x.shape
    x2d = x.reshape(b * t, d)
    tm = min(tm, b * t)
    q, k, lf, v, za, qb, kv, zb, gg = _in_projection(x2d, norm_in, lb_logits, w_in, tm)
    r3 = lambda a: a.reshape(b, t, a.shape[-1])
    ha, s_new = _hgrn(r3(q), r3(k), r3(v), r3(lf), r3(za), a_gnorm, hgrn_state,
                      tb=hgrn_tb, chunk=hgrn_chunk, sub=hgrn_sub, t_valid=t_valid)
    kv3 = r3(kv)
    if cache is None:
        hb = _swa_prompt(r3(qb), kv3, r3(zb), b_sinks)
        last = min(WINDOW, t)
        ck = kv3[:, t - last:, :B_KVW]
        cv = kv3[:, t - last:, B_KVW:]
    else:
        hb, ck, cv = _swa_sample(r3(qb), kv3, r3(zb), b_sinks, cache[0], cache[1], t_new=t_valid, seqs=8)
    y = _out_projection(x2d, ha.reshape(b * t, A_VW), hb.reshape(b * t, B_QW), gg, w_pa, w_pb, w_out, norm_final, tm)
    return y.reshape(b, t, d), s_new, ck, cv


def kernel(x_prompt, x_sample, state_hgrn, cache_k, cache_v, norm_in, w_in, lb_logits, a_gnorm, b_sinks,
           w_pa, w_pb, w_out, norm_final):
    assert state_hgrn.shape[0] == 1, "single decoder layer"
    weights = (norm_in[0], w_in[0].astype(BF16), lb_logits, a_gnorm[0], b_sinks[0],
               w_pa[0].astype(BF16), w_pb[0].astype(BF16), w_out[0].astype(BF16), norm_final)
    bp, tp, _ = x_prompt.shape
    y_p, s_p, ck_p, cv_p = _layer(x_prompt, weights, hgrn_state=None, cache=None, tm=256,
                                  hgrn_tb=min(512, tp), hgrn_chunk=HGRN_CHUNK, hgrn_sub=HGRN_SUB, t_valid=None)
    bs, ts, _ = x_sample.shape
    w = cache_k.shape[2]
    xs = jnp.pad(x_sample, ((0, 0), (0, SAMPLE_PAD - ts), (0, 0)))
    cache = (cache_k[0].reshape(bs, w, B_KVW), cache_v[0].reshape(bs, w, B_KVW))
    y_s, s_s, ck_s, cv_s = _layer(xs, weights, hgrn_state=state_hgrn[0], cache=cache, tm=256,
                                  hgrn_tb=SAMPLE_PAD, hgrn_chunk=SAMPLE_PAD, hgrn_sub=SAMPLE_PAD, t_valid=ts)
    kvshape = lambda a: a.reshape(1, a.shape[0], a.shape[1], B_KVHEADS, B_HDIM)
    return (y_p, y_s[:, :ts], s_p[None], kvshape(ck_p), kvshape(cv_p),
            s_s[None], kvshape(ck_s), kvshape(cv_s))
```

```python
import functools
import math

import jax
import jax.numpy as jnp
from jax import lax
from jax.experimental import pallas as pl
from jax.experimental.pallas import tpu as pltpu

F32 = jnp.float32
BF16 = jnp.bfloat16

D_MODEL = 1024
A_HEADS = 8
A_KDIM = 128
A_VDIM = D_MODEL // A_HEADS
A_KW = A_HEADS * A_KDIM
A_VW = A_HEADS * A_VDIM
B_QHEADS = 16
B_KVHEADS = 2
B_HDIM = 64
B_GROUP = B_QHEADS // B_KVHEADS
B_QW = B_QHEADS * B_HDIM
B_KVW = B_KVHEADS * B_HDIM
WINDOW = 128
EPS = 1e-6
NEG = -1e30
LANES = 128
SAMPLE_PAD = 16
Q_SLABS = B_QW // LANES
KV_SLABS = B_GROUP * B_HDIM // LANES

_SPLITS = (A_KW, A_KW, A_VW, A_VW, B_QW, B_KVW, B_KVW, B_QW, D_MODEL, D_MODEL)
_OFFS = tuple(int(sum(_SPLITS[:i])) for i in range(len(_SPLITS)))
IN_COLS = int(sum(_SPLITS))

VMEM_LIMIT = 56 * 1024 * 1024

HGRN_CHUNK = 64
HGRN_SUB = 16


def _sigmoid(x):
    return 1.0 / (1.0 + jnp.exp(-x))


def _silu(x):
    return x * _sigmoid(x)


def _dot_nt(a, b):
    return lax.dot_general(a, b, (((1,), (1,)), ((), ())), preferred_element_type=F32)


def _dot_tn(a, b, precision=None):
    return lax.dot_general(a, b, (((0,), (0,)), ((), ())), precision=precision, preferred_element_type=F32)


def _dot(a, b):
    return jnp.dot(a, b, preferred_element_type=F32)


def _log2(n):
    assert n & (n - 1) == 0
    return n.bit_length() - 1


def _inproj_kernel(x_ref, g_ref, lbl_ref, w_ref,
                   q_ref, k_ref, lf_ref, v_ref, za_ref, qb_ref, kv_ref, zb_ref, gg_ref):
    x = x_ref[...]
    ms = jnp.mean(x * x, axis=-1, keepdims=True)
    xn = (x * lax.rsqrt(ms + EPS) * g_ref[...]).astype(BF16)

    def seg(i, width=None):
        width = _SPLITS[i] if width is None else width
        return _dot(xn, w_ref[:, _OFFS[i]:_OFFS[i] + width])

    l = lbl_ref[...]
    e = jnp.exp(l - jnp.max(l, axis=0, keepdims=True))
    lb = e[0:1, :] / jnp.sum(e, axis=0, keepdims=True)

    q_ref[...] = _silu(seg(0)).astype(BF16)
    sig = _sigmoid(seg(1))
    k_ref[...] = ((1.0 - lb) * (1.0 - sig)).astype(BF16)
    lf_ref[...] = jnp.log(lb + (1.0 - lb) * sig)
    v_ref[...] = seg(2).astype(BF16)
    za_ref[...] = _silu(seg(3)).astype(BF16)
    qb = (seg(4) * Q_SCALE).astype(BF16)
    for p in range(Q_SLABS):
        qb_ref[p] = qb[:, p * LANES:(p + 1) * LANES]
    kv_ref[...] = seg(5, 2 * B_KVW)
    zb_ref[...] = _silu(seg(7)).astype(BF16)
    gg_ref[...] = _sigmoid(seg(8, 2 * D_MODEL)).astype(BF16)


def _in_projection(x2d, norm_in, lb_logits, w_in_bf16, tm):
    m = x2d.shape[0]
    assert m % tm == 0
    row = lambda w: pl.BlockSpec((tm, w), lambda i: (i, 0))
    const = lambda shape: pl.BlockSpec(shape, lambda i: (0, 0), pipeline_mode=pl.Buffered(1))
    out_shapes = (
        jax.ShapeDtypeStruct((m, A_KW), BF16),
        jax.ShapeDtypeStruct((m, A_KW), BF16),
        jax.ShapeDtypeStruct((m, A_KW), F32),
        jax.ShapeDtypeStruct((m, A_VW), BF16),
        jax.ShapeDtypeStruct((m, A_VW), BF16),
        jax.ShapeDtypeStruct((Q_SLABS, m, LANES), BF16),
        jax.ShapeDtypeStruct((m, 2 * B_KVW), F32),
        jax.ShapeDtypeStruct((m, B_QW), BF16),
        jax.ShapeDtypeStruct((m, 2 * D_MODEL), BF16),
    )
    out_specs = tuple(pl.BlockSpec((Q_SLABS, tm, LANES), lambda i: (0, i, 0)) if len(s.shape) == 3
                      else row(s.shape[1]) for s in out_shapes)
    return pl.pallas_call(
        _inproj_kernel,
        grid=(m // tm,),
        in_specs=[row(D_MODEL), const((1, D_MODEL)), const(lb_logits.shape), const((D_MODEL, IN_COLS))],
        out_specs=out_specs,
        out_shape=out_shapes,
        compiler_params=pltpu.CompilerParams(dimension_semantics=("arbitrary",), vmem_limit_bytes=VMEM_LIMIT),
        name="in_projection",
    )(x2d, norm_in.reshape(1, D_MODEL), lb_logits, w_in_bf16)


def _block_cumsum(x, sub):
    rows = lax.broadcasted_iota(jnp.int32, x.shape, 0) & (sub - 1)
    shift = 1
    while shift < sub:
        x = x + jnp.where(rows >= shift, pltpu.roll(x, shift, axis=0), 0.0)
        shift *= 2
    return x


def _rows(vecs, sub):
    return jnp.concatenate([jnp.broadcast_to(v, (sub, v.shape[1])) for v in vecs], axis=0)


def _hgrn_chunk(q, k, v, lf, st, chunk, sub, state_is_kv):
    _log2(sub)
    n = chunk // sub
    qf = q.astype(F32)
    kf = k.astype(F32)
    ell = _block_cumsum(lf, sub)
    tau = [ell[(i + 1) * sub - 1:(i + 1) * sub, :] for i in range(n)]
    mu = [ell[i * sub + sub // 2 - 1:i * sub + sub // 2, :] for i in range(n)]
    beta = [jnp.zeros_like(tau[0])]
    for i in range(n):
        beta.append(beta[i] + tau[i])
    total = beta[n]

    qt = qf * jnp.exp(ell)
    kt = kf * jnp.exp(_rows(tau, sub) - ell)
    mu_rows = _rows(mu, sub)
    qd = (qf * jnp.exp(ell - mu_rows)).astype(BF16)
    kd = (kf * jnp.exp(mu_rows - ell)).astype(BF16)
    qhat = (qt * _rows([jnp.exp(b) for b in beta[:n]], sub)).astype(BF16)
    khat = (kt * _rows([jnp.exp(total - b) for b in beta[1:]], sub)).astype(BF16)

    sb = st.astype(BF16)
    o = _dot(qhat, sb) if state_is_kv else _dot_nt(qhat, sb)

    r = lax.broadcasted_iota(jnp.int32, (chunk, chunk), 0)
    s = lax.broadcasted_iota(jnp.int32, (chunk, chunk), 1)
    att = jnp.where(s <= r, _dot_nt(qd, kd), 0.0)
    if n > 1:
        zero_blk = jnp.zeros((sub, A_KDIM), F32)
        off = [jnp.zeros((sub, chunk), F32)]
        for i in range(1, n):
            kin = [kt[j * sub:(j + 1) * sub] * jnp.exp(beta[i] - beta[j + 1]) for j in range(i)]
            kin = jnp.concatenate(kin + [zero_blk] * (n - i), axis=0).astype(BF16)
            off.append(_dot_nt(qt[i * sub:(i + 1) * sub].astype(BF16), kin))
        same = ((r ^ s) & ~(sub - 1)) == 0
        att = jnp.where(same, att, jnp.concatenate(off, axis=0))
    o = o + _dot(att.astype(BF16), v)
    if state_is_kv:
        total_col = _dot_tn(lf, jnp.ones_like(lf), precision=lax.Precision.HIGHEST)
        st_new = st * jnp.exp(total_col) + _dot_tn(khat, v)
    else:
        st_new = st * jnp.exp(total) + _dot_tn(v, khat)
    return o, st_new


def _hgrn_finish(o, gn, z):
    o = o * lax.rsqrt(jnp.mean(o * o, axis=-1, keepdims=True) + EPS) * gn
    return (o * z.astype(F32)).astype(BF16)


def _hgrn_prompt_kernel(q_ref, k_ref, v_ref, lf_ref, z_ref, gn_ref, o_ref, sout_ref, st_ref, *, chunk, sub):
    tb = q_ref.shape[1]
    nt = pl.program_id(1)

    @pl.when(nt == 0)
    def _():
        st_ref[...] = jnp.zeros_like(st_ref)

    gn = gn_ref[...]

    def body(ci, carry):
        rs = pl.ds(pl.multiple_of(ci * chunk, chunk), chunk)
        for h in range(A_HEADS):
            cs = slice(h * A_KDIM, (h + 1) * A_KDIM)
            o, st_new = _hgrn_chunk(q_ref[0, rs, cs], k_ref[0, rs, cs], v_ref[0, rs, cs], lf_ref[0, rs, cs],
                                    st_ref[h], chunk, sub, state_is_kv=False)
            st_ref[h] = st_new
            o_ref[0, rs, cs] = _hgrn_finish(o, gn, z_ref[0, rs, cs])
        return carry

    lax.fori_loop(0, tb // chunk, body, 0)

    @pl.when(nt == pl.num_programs(1) - 1)
    def _():
        for h in range(A_HEADS):
            sout_ref[0, h] = st_ref[h].T


def _hgrn_prompt(q, k, v, lf, z, a_gnorm, *, tb, chunk, sub):
    b, t, _ = q.shape
    assert t % tb == 0 and tb % chunk == 0 and chunk % sub == 0
    blk = lambda w: pl.BlockSpec((1, tb, w), lambda i, j: (i, j, 0))
    return pl.pallas_call(
        functools.partial(_hgrn_prompt_kernel, chunk=chunk, sub=sub),
        grid=(b, t // tb),
        in_specs=[blk(A_KW), blk(A_KW), blk(A_VW), blk(A_KW), blk(A_VW),
                  pl.BlockSpec((1, A_VDIM), lambda i, j: (0, 0))],
        out_specs=(blk(A_VW), pl.BlockSpec((1, A_HEADS, A_KDIM, A_VDIM), lambda i, j: (i, 0, 0, 0))),
        out_shape=(jax.ShapeDtypeStruct((b, t, A_VW), BF16),
                   jax.ShapeDtypeStruct((b, A_HEADS, A_KDIM, A_VDIM), F32)),
        scratch_shapes=[pltpu.VMEM((A_HEADS, A_VDIM, A_KDIM), F32)],
        compiler_params=pltpu.CompilerParams(dimension_semantics=("arbitrary", "arbitrary"),
                                             vmem_limit_bytes=VMEM_LIMIT),
        name="hgrn2_prompt",
    )(q, k, v, lf, z, a_gnorm.reshape(1, A_VDIM))


def _hgrn_sample_kernel(q_ref, k_ref, v_ref, lf_ref, z_ref, gn_ref, s0_ref, o_ref, sout_ref, *, t_valid, seqs):
    tp = q_ref.shape[1]
    gn = gn_ref[...]
    rows = lax.broadcasted_iota(jnp.int32, (tp, A_KDIM), 0)
    for b in range(seqs):
        for h in range(A_HEADS):
            cs = slice(h * A_KDIM, (h + 1) * A_KDIM)
            lf = jnp.where(rows < t_valid, lf_ref[b, :, cs], 0.0)
            o, st_new = _hgrn_chunk(q_ref[b, :, cs], k_ref[b, :, cs], v_ref[b, :, cs], lf, s0_ref[b, h],
                                    tp, tp, state_is_kv=True)
            sout_ref[b, h] = st_new
            o_ref[b, :, cs] = _hgrn_finish(o, gn, z_ref[b, :, cs])


def _hgrn_sample(q, k, v, lf, z, a_gnorm, state, *, t_valid, seqs):
    b, tp, _ = q.shape
    assert b % seqs == 0
    blk = lambda w: pl.BlockSpec((seqs, tp, w), lambda i: (i, 0, 0))
    sblk = pl.BlockSpec((seqs, A_HEADS, A_KDIM, A_VDIM), lambda i: (i, 0, 0, 0))
    return pl.pallas_call(
        functools.partial(_hgrn_sample_kernel, t_valid=t_valid, seqs=seqs),
        grid=(b // seqs,),
        in_specs=[blk(A_KW), blk(A_KW), blk(A_VW), blk(A_KW), blk(A_VW),
                  pl.BlockSpec((1, A_VDIM), lambda i: (0, 0)), sblk],
        out_specs=(blk(A_VW), sblk),
        out_shape=(jax.ShapeDtypeStruct((b, tp, A_VW), BF16),
                   jax.ShapeDtypeStruct((b, A_HEADS, A_KDIM, A_VDIM), F32)),
        compiler_params=pltpu.CompilerParams(dimension_semantics=("arbitrary",), vmem_limit_bytes=VMEM_LIMIT),
        name="hgrn2_sample",
    )(q, k, v, lf, z, a_gnorm.reshape(1, A_VDIM), state)


LOG2E = math.log2(math.e)
Q_SCALE = LOG2E / math.sqrt(B_HDIM)


def _alibi_slope(h):
    return 2.0 ** (-8.0 * (h + 1) / B_QHEADS)


def _lane_halves(x):
    low = lax.broadcasted_iota(jnp.int32, x.shape, 1) < B_HDIM
    swapped = pltpu.roll(x, B_HDIM, axis=1)
    both = (jnp.where(low, x, swapped), jnp.where(low, swapped, x))
    return [[jnp.where(low, h, 0.0).astype(BF16), jnp.where(low, 0.0, h).astype(BF16)] for h in both]


def _per_slab(slab, values):
    out = jnp.full(slab.shape, values[0], F32)
    for p in range(1, len(values)):
        out = jnp.where(slab == p, values[p], out)
    return out


def _sink_slot(tile, par):
    head = tile[:16]
    row0 = lax.broadcasted_iota(jnp.int32, head.shape, 0) == 0
    low = lax.broadcasted_iota(jnp.int32, head.shape, 1) < B_HDIM
    feat = low if par == 0 else jnp.logical_not(low)
    return jnp.concatenate([jnp.where(row0 & feat, jnp.zeros_like(head), head), tile[16:]], axis=0)


def _bias_table(kvh, par, sinks_ref, rows_per_slab, dist, valid, sink_in_col0):
    shift = _log2(rows_per_slab)
    heads = [kvh * B_GROUP + 2 * p + par for p in range(KV_SLABS)]
    slab = lax.broadcasted_iota(jnp.int32, dist.shape, 0) >> shift
    bias = jnp.where(valid, -_per_slab(slab, [_alibi_slope(h) * LOG2E for h in heads]) * dist.astype(F32), NEG)
    if sink_in_col0:
        col = lax.broadcasted_iota(jnp.int32, dist.shape, 1)
        bias = jnp.where(col == 0, _per_slab(slab, [sinks_ref[h] * LOG2E for h in heads]), bias)
    return bias


ROW_TILE = 128


def _attend(q4, keys, values, bias_fns, emit):
    rows = q4.shape[0]
    tile = min(ROW_TILE, rows)
    tiles = [slice(r, r + tile) for r in range(0, rows, tile)]
    acc = []
    for par in range(2):
        scores = [_dot_nt(q4, ks[par]) for ks in keys]
        probs = [[] for _ in keys]
        for rs in tiles:
            st = [s[rs] + bf(par, rs) for s, bf in zip(scores, bias_fns)]
            m = None
            for s in st:
                full = [s[:, c:c + LANES] for c in range(0, s.shape[1] - LANES + 1, LANES)]
                mx = jnp.max(functools.reduce(jnp.maximum, full) if full else s, axis=-1, keepdims=True)
                m = mx if m is None else jnp.maximum(m, mx)
            for i, s in enumerate(st):
                probs[i].append(jnp.exp2(s - m).astype(BF16))
        a = None
        for i, vs in enumerate(values):
            v_ones = jnp.concatenate([vs[par], jnp.ones_like(vs[par])], axis=1)
            part = _dot(jnp.concatenate(probs[i], axis=0), v_ones)
            a = part if a is None else a + part
        acc.append(a)
    for rs in tiles:
        emit(rs.start, acc[0][rs, :LANES] / acc[0][rs, LANES:] + acc[1][rs, :LANES] / acc[1][rs, LANES:])


def _swa_prompt_kernel(sinks_ref, q_ref, kvp_ref, kvc_ref, z_ref, o_ref, bias_ref, *, nblk):
    w = WINDOW
    j = pl.program_id(1)

    @pl.when((pl.program_id(0) == 0) & (j == 0))
    def _():
        shape = (KV_SLABS * w, 2 * w)
        qi = lax.broadcasted_iota(jnp.int32, shape, 0) & (w - 1)
        kj = lax.broadcasted_iota(jnp.int32, shape, 1) - w
        dist = qi - kj
        band = (dist >= 0) & (dist < w)
        for kvh in range(B_KVHEADS):
            for par in range(2):
                bias_ref[0, kvh, par] = _bias_table(kvh, par, sinks_ref, w, dist, band & (kj >= 0), True)
                bias_ref[1, kvh, par] = _bias_table(kvh, par, sinks_ref, w, dist, band, True)

    def halves(kv):
        return _lane_halves(kv[:, :B_KVW]), _lane_halves(kv[:, B_KVW:])

    prev = halves(kvp_ref[0])
    for i in range(nblk):
        rs = slice(i * w, (i + 1) * w)
        cur = halves(kvc_ref[0, rs, :])
        sel = jnp.minimum(j, 1) if i == 0 else 1
        for kvh in range(B_KVHEADS):
            q4 = q_ref[kvh * KV_SLABS:(kvh + 1) * KV_SLABS, rs, :].reshape(KV_SLABS * w, LANES)
            kh, vh = [[jnp.concatenate([_sink_slot(prev[x][kvh][par], par), cur[x][kvh][par]], axis=0)
                       for par in range(2)] for x in range(2)]

            def emit(r0, o, kvh=kvh, t0=i * w):
                p, t = divmod(r0, w)
                cs = slice((kvh * KV_SLABS + p) * LANES, (kvh * KV_SLABS + p + 1) * LANES)
                tr = slice(t0 + t, t0 + t + o.shape[0])
                o_ref[0, tr, cs] = (o * z_ref[0, tr, cs].astype(F32)).astype(BF16)

            _attend(q4, [kh], [vh], [lambda par, rows, kvh=kvh, sel=sel: bias_ref[sel, kvh, par, rows, :]], emit)
        prev = cur


def _swa_prompt(qb, kv, zb, sinks, *, nblk):
    b, t, _ = zb.shape
    tq = nblk * WINDOW
    assert t % tq == 0
    nq = t // tq
    return pl.pallas_call(
        functools.partial(_swa_prompt_kernel, nblk=nblk),
        grid=(b, nq),
        in_specs=[pl.BlockSpec(memory_space=pltpu.SMEM),
                  pl.BlockSpec((Q_SLABS, tq, LANES), lambda i, j: (0, i * nq + j, 0)),
                  pl.BlockSpec((1, WINDOW, 2 * B_KVW), lambda i, j: (i, jnp.maximum(j * nblk - 1, 0), 0)),
                  pl.BlockSpec((1, tq, 2 * B_KVW), lambda i, j: (i, j, 0)),
                  pl.BlockSpec((1, tq, B_QW), lambda i, j: (i, j, 0))],
        out_specs=pl.BlockSpec((1, tq, B_QW), lambda i, j: (i, j, 0)),
        out_shape=jax.ShapeDtypeStruct((b, t, B_QW), BF16),
        scratch_shapes=[pltpu.VMEM((2, B_KVHEADS, 2, KV_SLABS * WINDOW, 2 * WINDOW), F32)],
        compiler_params=pltpu.CompilerParams(dimension_semantics=("arbitrary", "arbitrary"),
                                             vmem_limit_bytes=VMEM_LIMIT),
        name="swa_prompt",
    )(sinks, qb, kv, kv, zb)


def _swa_sample_kernel(sinks_ref, q_ref, kvn_ref, ck_ref, cv_ref, z_ref, o_ref, cko_ref, cvo_ref,
                       biasc_ref, biasn_ref, *, t_new, seqs):
    w = ck_ref.shape[1]
    tp = kvn_ref.shape[1]
    assert 0 < t_new < WINDOW <= w

    @pl.when(pl.program_id(0) == 0)
    def _():
        t_c = lax.broadcasted_iota(jnp.int32, (KV_SLABS * tp, w), 0) & (tp - 1)
        d_c = t_c - (lax.broadcasted_iota(jnp.int32, (KV_SLABS * tp, w), 1) - w)
        t_n = lax.broadcasted_iota(jnp.int32, (KV_SLABS * tp, tp), 0) & (tp - 1)
        s_n = lax.broadcasted_iota(jnp.int32, (KV_SLABS * tp, tp), 1)
        d_n = t_n - s_n
        for kvh in range(B_KVHEADS):
            for par in range(2):
                biasc_ref[kvh, par] = _bias_table(kvh, par, sinks_ref, tp, d_c, (d_c >= 0) & (d_c < WINDOW), True)
                biasn_ref[kvh, par] = _bias_table(kvh, par, sinks_ref, tp, d_n,
                                                  (d_n >= 0) & (d_n < WINDOW) & (s_n < t_new), False)

    for b in range(seqs):
        ck = ck_ref[b]
        cv = cv_ref[b]
        kn = kvn_ref[b, :, :B_KVW]
        vn = kvn_ref[b, :, B_KVW:]
        kch, vch = [[[_sink_slot(t, par) for par, t in enumerate(pair)] for pair in _lane_halves(x)]
                    for x in (ck, cv)]
        knh, vnh = _lane_halves(kn), _lane_halves(vn)
        for kvh in range(B_KVHEADS):
            q4 = q_ref[kvh * KV_SLABS:(kvh + 1) * KV_SLABS, b * tp:(b + 1) * tp, :].reshape(KV_SLABS * tp, LANES)

            def emit(r0, o, kvh=kvh, b=b):
                for r in range(0, o.shape[0], tp):
                    p = (r0 + r) // tp
                    cs = slice((kvh * KV_SLABS + p) * LANES, (kvh * KV_SLABS + p + 1) * LANES)
                    o_ref[b, :, cs] = (o[r:r + tp] * z_ref[b, :, cs].astype(F32)).astype(BF16)

            _attend(q4, [kch[kvh], knh[kvh]], [vch[kvh], vnh[kvh]],
                    [lambda par, rows, kvh=kvh: biasc_ref[kvh, par, rows, :],
                     lambda par, rows, kvh=kvh: biasn_ref[kvh, par, rows, :]], emit)

        cko_ref[b, 0:w - t_new, :] = ck[t_new:w, :]
        cko_ref[b, w - t_new:w, :] = kn[0:t_new, :]
        cvo_ref[b, 0:w - t_new, :] = cv[t_new:w, :]
        cvo_ref[b, w - t_new:w, :] = vn[0:t_new, :]


def _swa_sample(qb, kvn, zb, sinks, cache_k, cache_v, *, t_new, seqs):
    b, tp, _ = zb.shape
    w = cache_k.shape[1]
    assert b % seqs == 0
    blk = lambda r, c: pl.BlockSpec((seqs, r, c), lambda i: (i, 0, 0))
    tbl = lambda s: pltpu.VMEM((B_KVHEADS, 2, KV_SLABS * tp, s), F32)
    return pl.pallas_call(
        functools.partial(_swa_sample_kernel, t_new=t_new, seqs=seqs),
        grid=(b // seqs,),
        in_specs=[pl.BlockSpec(memory_space=pltpu.SMEM),
                  pl.BlockSpec((Q_SLABS, seqs * tp, LANES), lambda i: (0, i, 0)),
                  blk(tp, 2 * B_KVW), blk(w, B_KVW), blk(w, B_KVW), blk(tp, B_QW)],
        out_specs=(blk(tp, B_QW), blk(w, B_KVW), blk(w, B_KVW)),
        out_shape=(jax.ShapeDtypeStruct((b, tp, B_QW), BF16),
                   jax.ShapeDtypeStruct((b, w, B_KVW), F32),
                   jax.ShapeDtypeStruct((b, w, B_KVW), F32)),
        scratch_shapes=[tbl(w), tbl(tp)],
        compiler_params=pltpu.CompilerParams(dimension_semantics=("arbitrary",), vmem_limit_bytes=VMEM_LIMIT),
        name="swa_sample",
    )(sinks, qb, kvn, cache_k, cache_v, zb)


def _outproj_kernel(x_ref, ha_ref, hb_ref, gg_ref, wpa_ref, wpb_ref, wo_ref, gf_ref, y_ref):
    pa = _dot(ha_ref[...], wpa_ref[...])
    pb = _dot(hb_ref[...], wpb_ref[...])
    merged = gg_ref[:, :D_MODEL].astype(F32) * pa + gg_ref[:, D_MODEL:].astype(F32) * pb
    y = x_ref[...] + _dot(merged.astype(BF16), wo_ref[...])
    ms = jnp.mean(y * y, axis=-1, keepdims=True)
    y_ref[...] = y * lax.rsqrt(ms + EPS) * gf_ref[...]


def _out_projection(x2d, ha, hb, gg, w_pa, w_pb, w_out, norm_final, tm):
    m = x2d.shape[0]
    assert m % tm == 0
    row = lambda w: pl.BlockSpec((tm, w), lambda i: (i, 0))
    const = lambda shape: pl.BlockSpec(shape, lambda i: (0, 0), pipeline_mode=pl.Buffered(1))
    return pl.pallas_call(
        _outproj_kernel,
        grid=(m // tm,),
        in_specs=[row(D_MODEL), row(A_VW), row(B_QW), row(2 * D_MODEL),
                  const((A_VW, D_MODEL)), const((B_QW, D_MODEL)), const((D_MODEL, D_MODEL)), const((1, D_MODEL))],
        out_specs=row(D_MODEL),
        out_shape=jax.ShapeDtypeStruct((m, D_MODEL), F32),
        compiler_params=pltpu.CompilerParams(dimension_semantics=("arbitrary",), vmem_limit_bytes=VMEM_LIMIT),
        name="out_projection",
    )(x2d, ha, hb, gg, w_pa, w_pb, w_out, norm_final.reshape(1, D_MODEL))


def _layer(x, weights, *, hgrn_state, cache, t_valid):
    norm_in, w_in, lb_logits, a_gnorm, b_sinks, w_pa, w_pb, w_out, norm_final = weights
    b, t, d = x.shape
    x2d = x.reshape(b * t, d)
    tm = min(256, b * t)
    q, k, lf, v, za, qb, kv, zb, gg = _in_projection(x2d, norm_in, lb_logits, w_in, tm)
    r3 = lambda a: a.reshape(b, t, a.shape[-1])
    kv3 = r3(kv)
    if cache is None:
        ha, s_new = _hgrn_prompt(r3(q), r3(k), r3(v), r3(lf), r3(za), a_gnorm,
                                 tb=min(512, t), chunk=HGRN_CHUNK, sub=HGRN_SUB)
        hb = _swa_prompt(qb, kv3, r3(zb), b_sinks, nblk=min(4, t // WINDOW))
        last = min(WINDOW, t)
        ck = kv3[:, t - last:, :B_KVW]
        cv = kv3[:, t - last:, B_KVW:]
    else:
        ha, s_new = _hgrn_sample(r3(q), r3(k), r3(v), r3(lf), r3(za), a_gnorm, hgrn_state,
                                 t_valid=t_valid, seqs=min(2, b))
        hb, ck, cv = _swa_sample(qb, kv3, r3(zb), b_sinks, cache[0], cache[1], t_new=t_valid, seqs=min(8, b))
    y = _out_projection(x2d, ha.reshape(b * t, A_VW), hb.reshape(b * t, B_QW), gg, w_pa, w_pb, w_out, norm_final, tm)
    return y.reshape(b, t, d), s_new, ck, cv


def kernel(x_prompt, x_sample, state_hgrn, cache_k, cache_v, norm_in, w_in, lb_logits, a_gnorm, b_sinks,
           w_pa, w_pb, w_out, norm_final):
    assert state_hgrn.shape[0] == 1, "single decoder layer"
    weights = (norm_in[0], w_in[0].astype(BF16), lb_logits, a_gnorm[0], b_sinks[0],
               w_pa[0].astype(BF16), w_pb[0].astype(BF16), w_out[0].astype(BF16), norm_final)
    y_p, s_p, ck_p, cv_p = _layer(x_prompt, weights, hgrn_state=None, cache=None, t_valid=None)
    bs, ts, _ = x_sample.shape
    w = cache_k.shape[2]
    xs = jnp.pad(x_sample, ((0, 0), (0, SAMPLE_PAD - ts), (0, 0)))
    cache = (cache_k[0].reshape(bs, w, B_KVW), cache_v[0].reshape(bs, w, B_KVW))
    y_s, s_s, ck_s, cv_s = _layer(xs, weights, hgrn_state=state_hgrn[0], cache=cache, t_valid=ts)
    kvshape = lambda a: a.reshape(1, a.shape[0], a.shape[1], B_KVHEADS, B_HDIM)
    return (y_p, y_s[:, :ts], s_p[None], kvshape(ck_p), kvshape(cv_p),
            s_s[None], kvshape(ck_s), kvshape(cv_s))
```

```python
import functools
import math

import jax
import jax.numpy as jnp
from jax import lax
from jax.experimental import pallas as pl
from jax.experimental.pallas import tpu as pltpu

F32 = jnp.float32
BF16 = jnp.bfloat16

D_MODEL = 1024
A_HEADS = 8
A_KDIM = 128
A_VDIM = D_MODEL // A_HEADS
A_KW = A_HEADS * A_KDIM
A_VW = A_HEADS * A_VDIM
B_QHEADS = 16
B_KVHEADS = 2
B_HDIM = 64
B_GROUP = B_QHEADS // B_KVHEADS
B_QW = B_QHEADS * B_HDIM
B_KVW = B_KVHEADS * B_HDIM
WINDOW = 128
EPS = 1e-6
NEG = -1e30
LANES = 128
SAMPLE_PAD = 16
Q_SLABS = B_QW // LANES
KV_SLABS = B_GROUP * B_HDIM // LANES

_SPLITS = (A_KW, A_KW, A_VW, A_VW, B_QW, B_KVW, B_KVW, B_QW, D_MODEL, D_MODEL)
_OFFS = tuple(int(sum(_SPLITS[:i])) for i in range(len(_SPLITS)))
IN_COLS = int(sum(_SPLITS))

VMEM_LIMIT = 56 * 1024 * 1024

TM_IN = 256
TM_OUT = 512
HGRN_CHUNK = 64
HGRN_SUB = 16


def _sigmoid(x):
    return 1.0 / (1.0 + jnp.exp(-x))


def _silu(x):
    return x * _sigmoid(x)


def _dot_nt(a, b):
    return lax.dot_general(a, b, (((1,), (1,)), ((), ())), preferred_element_type=F32)


def _dot_tn(a, b, precision=None):
    return lax.dot_general(a, b, (((0,), (0,)), ((), ())), precision=precision, preferred_element_type=F32)


def _dot(a, b):
    return jnp.dot(a, b, preferred_element_type=F32)


def _log2(n):
    assert n & (n - 1) == 0
    return n.bit_length() - 1


def _inproj_kernel(x_ref, g_ref, lbl_ref, gn_ref, w_ref,
                   q_ref, k_ref, lf_ref, v_ref, za_ref, qb_ref, kv_ref, zb_ref, gg_ref):
    x = x_ref[...]
    ms = jnp.mean(x * x, axis=-1, keepdims=True)
    xn = (x * lax.rsqrt(ms + EPS) * g_ref[...]).astype(BF16)

    def seg(i, width=None):
        width = _SPLITS[i] if width is None else width
        return _dot(xn, w_ref[:, _OFFS[i]:_OFFS[i] + width])

    l = lbl_ref[...]
    e = jnp.exp(l - jnp.max(l, axis=0, keepdims=True))
    lb = e[0:1, :] / jnp.sum(e, axis=0, keepdims=True)

    q_ref[...] = _silu(seg(0)).astype(BF16)
    sig = _sigmoid(seg(1))
    k_ref[...] = ((1.0 - lb) * (1.0 - sig)).astype(BF16)
    lf_ref[...] = jnp.log2(lb + (1.0 - lb) * sig)
    v_ref[...] = seg(2).astype(BF16)
    za_ref[...] = (_silu(seg(3)) * gn_ref[...]).astype(BF16)
    qb = (seg(4) * Q_SCALE).astype(BF16)
    for p in range(Q_SLABS):
        qb_ref[p] = qb[:, p * LANES:(p + 1) * LANES]
    kv_ref[...] = seg(5, 2 * B_KVW)
    zb_ref[...] = _silu(seg(7)).astype(BF16)
    gg_ref[...] = _sigmoid(seg(8, 2 * D_MODEL)).astype(BF16)


def _in_projection(x2d, norm_in, lb_logits, a_gnorm, w_in_bf16, tm):
    m = x2d.shape[0]
    assert m % tm == 0
    row = lambda w: pl.BlockSpec((tm, w), lambda i: (i, 0))
    const = lambda shape: pl.BlockSpec(shape, lambda i: (0, 0), pipeline_mode=pl.Buffered(1))
    out_shapes = (
        jax.ShapeDtypeStruct((m, A_KW), BF16),
        jax.ShapeDtypeStruct((m, A_KW), BF16),
        jax.ShapeDtypeStruct((m, A_KW), F32),
        jax.ShapeDtypeStruct((m, A_VW), BF16),
        jax.ShapeDtypeStruct((m, A_VW), BF16),
        jax.ShapeDtypeStruct((Q_SLABS, m, LANES), BF16),
        jax.ShapeDtypeStruct((m, 2 * B_KVW), F32),
        jax.ShapeDtypeStruct((m, B_QW), BF16),
        jax.ShapeDtypeStruct((m, 2 * D_MODEL), BF16),
    )
    out_specs = tuple(pl.BlockSpec((Q_SLABS, tm, LANES), lambda i: (0, i, 0)) if len(s.shape) == 3
                      else row(s.shape[1]) for s in out_shapes)
    return pl.pallas_call(
        _inproj_kernel,
        grid=(m // tm,),
        in_specs=[row(D_MODEL), const((1, D_MODEL)), const(lb_logits.shape), const((1, A_VW)),
                  const((D_MODEL, IN_COLS))],
        out_specs=out_specs,
        out_shape=out_shapes,
        compiler_params=pltpu.CompilerParams(dimension_semantics=("arbitrary",), vmem_limit_bytes=VMEM_LIMIT),
        name="in_projection",
    )(x2d, norm_in.reshape(1, D_MODEL), lb_logits, jnp.tile(a_gnorm, A_HEADS).reshape(1, A_VW), w_in_bf16)


def _rows(vecs, sub):
    return jnp.concatenate([jnp.broadcast_to(v, (sub, v.shape[1])) for v in vecs], axis=0)


def _tri_blocks(chunk, sub):
    r = lax.broadcasted_iota(jnp.int32, (chunk, chunk), 0)
    s = lax.broadcasted_iota(jnp.int32, (chunk, chunk), 1)
    return jnp.where((s <= r) & (((r ^ s) & ~(sub - 1)) == 0), 1.0, 0.0).astype(BF16)


def _block_cumsum_mxu(x, tri):
    hi = x.astype(BF16)
    lo = (x - hi.astype(F32)).astype(BF16)
    return _dot(tri, hi) + _dot(tri, lo)


def _hgrn_operands(q, k, ell, chunk, sub):
    _log2(sub)
    n = chunk // sub
    qf = q.astype(F32)
    kf = k.astype(F32)
    tau = [ell[(i + 1) * sub - 1:(i + 1) * sub, :] for i in range(n)]
    mu = [ell[i * sub + sub // 2 - 1:i * sub + sub // 2, :] for i in range(n)]
    beta = [jnp.zeros_like(tau[0])]
    for i in range(n):
        beta.append(beta[i] + tau[i])
    total = beta[n]

    def brow(x):
        return jnp.broadcast_to(x, (sub, x.shape[1])).astype(BF16)

    qt = (qf * jnp.exp2(ell)).astype(BF16)
    kt = (kf * jnp.exp2(_rows(tau, sub) - ell)).astype(BF16)
    mu_rows = _rows(mu, sub)
    qd = (qf * jnp.exp2(ell - mu_rows)).astype(BF16)
    kd = (kf * jnp.exp2(mu_rows - ell)).astype(BF16)
    blk = lambda x, i: x[i * sub:(i + 1) * sub]
    qhat = jnp.concatenate([blk(qt, i) * brow(jnp.exp2(beta[i])) for i in range(n)], axis=0)
    khat = jnp.concatenate([blk(kt, i) * brow(jnp.exp2(total - beta[i + 1])) for i in range(n)], axis=0)
    zero_blk = jnp.zeros((sub, A_KDIM), BF16)
    qts, kins = [], []
    for i in range(1, n):
        kin = [blk(kt, j) * brow(jnp.exp2(beta[i] - beta[j + 1])) for j in range(i)]
        kins.append(jnp.concatenate(kin + [zero_blk] * (n - i), axis=0))
        qts.append(blk(qt, i))
    return dict(qd=qd, kd=kd, qhat=qhat, khat=khat, qts=qts, kins=kins, total=total)


def _hgrn_scores(ops, st, masks, state_is_kv):
    causal, same = masks
    sb = st.astype(BF16)
    o_inter = _dot(ops["qhat"], sb) if state_is_kv else _dot_nt(ops["qhat"], sb)
    att = jnp.where(causal, _dot_nt(ops["qd"], ops["kd"]), 0.0)
    if ops["qts"]:
        sub, chunk = ops["qts"][0].shape[0], att.shape[0]
        off = [jnp.zeros((sub, chunk), F32)] + [_dot_nt(qt, kin) for qt, kin in zip(ops["qts"], ops["kins"])]
        att = jnp.where(same, att, jnp.concatenate(off, axis=0))
    return o_inter, att.astype(BF16)


def _hgrn_update(ops, scores, v, st, lf, state_is_kv):
    o_inter, att = scores
    o = o_inter + _dot(att, v)
    if state_is_kv:
        total_col = _dot_tn(lf, jnp.ones_like(lf), precision=lax.Precision.HIGHEST)
        st_new = st * jnp.exp2(total_col) + _dot_tn(ops["khat"], v)
    else:
        st_new = st * jnp.exp2(ops["total"]) + _dot_tn(v, ops["khat"])
    return o, st_new


def _chunk_masks(chunk, sub):
    r = lax.broadcasted_iota(jnp.int32, (chunk, chunk), 0)
    s = lax.broadcasted_iota(jnp.int32, (chunk, chunk), 1)
    return s <= r, ((r ^ s) & ~(sub - 1)) == 0


def _hgrn_finish(o, z):
    return (o * lax.rsqrt(jnp.mean(o * o, axis=-1, keepdims=True) + EPS) * z.astype(F32)).astype(BF16)


def _hgrn_prompt_kernel(q_ref, k_ref, v_ref, lf_ref, z_ref, o_ref, sout_ref, st_ref, *, chunk, sub):
    tb = q_ref.shape[1]
    nt = pl.program_id(1)

    @pl.when(nt == 0)
    def _():
        st_ref[...] = jnp.zeros_like(st_ref)

    tri = _tri_blocks(chunk, sub)
    masks = _chunk_masks(chunk, sub)
    cols = [slice(h * A_KDIM, (h + 1) * A_KDIM) for h in range(A_HEADS)]

    def body(ci, carry):
        rs = pl.ds(pl.multiple_of(ci * chunk, chunk), chunk)
        ell = _block_cumsum_mxu(lf_ref[0, rs, :], tri)
        ops = [_hgrn_operands(q_ref[0, rs, cs], k_ref[0, rs, cs], ell[:, cs], chunk, sub) for cs in cols]
        scores = [_hgrn_scores(ops[h], st_ref[h], masks, False) for h in range(A_HEADS)]
        outs = []
        for h, cs in enumerate(cols):
            o, st_ref[h] = _hgrn_update(ops[h], scores[h], v_ref[0, rs, cs], st_ref[h], None, False)
            outs.append(o)
        for o, cs in zip(outs, cols):
            o_ref[0, rs, cs] = _hgrn_finish(o, z_ref[0, rs, cs])
        return carry

    lax.fori_loop(0, tb // chunk, body, 0, unroll=4)

    @pl.when(nt == pl.num_programs(1) - 1)
    def _():
        for h in range(A_HEADS):
            sout_ref[0, h] = st_ref[h].T


def _hgrn_prompt(q, k, v, lf, z, *, tb, chunk, sub):
    b, t, _ = q.shape
    assert t % tb == 0 and tb % chunk == 0 and chunk % sub == 0
    blk = lambda w: pl.BlockSpec((1, tb, w), lambda i, j: (i, j, 0))
    return pl.pallas_call(
        functools.partial(_hgrn_prompt_kernel, chunk=chunk, sub=sub),
        grid=(b, t // tb),
        in_specs=[blk(A_KW), blk(A_KW), blk(A_VW), blk(A_KW), blk(A_VW)],
        out_specs=(blk(A_VW), pl.BlockSpec((1, A_HEADS, A_KDIM, A_VDIM), lambda i, j: (i, 0, 0, 0))),
        out_shape=(jax.ShapeDtypeStruct((b, t, A_VW), BF16),
                   jax.ShapeDtypeStruct((b, A_HEADS, A_KDIM, A_VDIM), F32)),
        scratch_shapes=[pltpu.VMEM((A_HEADS, A_VDIM, A_KDIM), F32)],
        compiler_params=pltpu.CompilerParams(dimension_semantics=("arbitrary", "arbitrary"),
                                             vmem_limit_bytes=VMEM_LIMIT),
        name="hgrn2_prompt",
    )(q, k, v, lf, z)


def _hgrn_sample_kernel(q_ref, k_ref, v_ref, lf_ref, z_ref, s0_ref, o_ref, sout_ref, *, t_valid, seqs):
    tp = q_ref.shape[1]
    tri = _tri_blocks(tp, tp)
    masks = _chunk_masks(tp, tp)
    rows = lax.broadcasted_iota(jnp.int32, (tp, A_KW), 0)
    units = [(b, h, slice(h * A_KDIM, (h + 1) * A_KDIM)) for b in range(seqs) for h in range(A_HEADS)]
    lfs = [jnp.where(rows < t_valid, lf_ref[b], 0.0) for b in range(seqs)]
    ells = [_block_cumsum_mxu(lf, tri) for lf in lfs]
    ops = [_hgrn_operands(q_ref[b, :, cs], k_ref[b, :, cs], ells[b][:, cs], tp, tp) for b, h, cs in units]
    scores = [_hgrn_scores(op, s0_ref[b, h], masks, True) for op, (b, h, cs) in zip(ops, units)]
    outs = []
    for op, sc, (b, h, cs) in zip(ops, scores, units):
        o, sout_ref[b, h] = _hgrn_update(op, sc, v_ref[b, :, cs], s0_ref[b, h], lfs[b][:, cs], True)
        outs.append(o)
    for o, (b, h, cs) in zip(outs, units):
        o_ref[b, :, cs] = _hgrn_finish(o, z_ref[b, :, cs])


def _hgrn_sample(q, k, v, lf, z, state, *, t_valid, seqs):
    b, tp, _ = q.shape
    assert b % seqs == 0
    blk = lambda w: pl.BlockSpec((seqs, tp, w), lambda i: (i, 0, 0))
    sblk = pl.BlockSpec((seqs, A_HEADS, A_KDIM, A_VDIM), lambda i: (i, 0, 0, 0))
    return pl.pallas_call(
        functools.partial(_hgrn_sample_kernel, t_valid=t_valid, seqs=seqs),
        grid=(b // seqs,),
        in_specs=[blk(A_KW), blk(A_KW), blk(A_VW), blk(A_KW), blk(A_VW), sblk],
        out_specs=(blk(A_VW), sblk),
        out_shape=(jax.ShapeDtypeStruct((b, tp, A_VW), BF16),
                   jax.ShapeDtypeStruct((b, A_HEADS, A_KDIM, A_VDIM), F32)),
        compiler_params=pltpu.CompilerParams(dimension_semantics=("arbitrary",), vmem_limit_bytes=VMEM_LIMIT),
        name="hgrn2_sample",
    )(q, k, v, lf, z, state)


LOG2E = math.log2(math.e)
Q_SCALE = LOG2E / math.sqrt(B_HDIM)


def _alibi_slope(h):
    return 2.0 ** (-8.0 * (h + 1) / B_QHEADS)


def _lane_halves(x):
    low = lax.broadcasted_iota(jnp.int32, x.shape, 1) < B_HDIM
    swapped = pltpu.roll(x, B_HDIM, axis=1)
    both = (jnp.where(low, x, swapped), jnp.where(low, swapped, x))
    return [[jnp.where(low, h, 0.0).astype(BF16), jnp.where(low, 0.0, h).astype(BF16)] for h in both]


def _per_slab(slab, values):
    out = jnp.full(slab.shape, values[0], F32)
    for p in range(1, len(values)):
        out = jnp.where(slab == p, values[p], out)
    return out


def _sink_slot(tile, par):
    head = tile[:16]
    row0 = lax.broadcasted_iota(jnp.int32, head.shape, 0) == 0
    low = lax.broadcasted_iota(jnp.int32, head.shape, 1) < B_HDIM
    feat = low if par == 0 else jnp.logical_not(low)
    return jnp.concatenate([jnp.where(row0 & feat, jnp.zeros_like(head), head), tile[16:]], axis=0)


def _bias_table(kvh, par, sinks_ref, rows_per_slab, dist, valid, sink_in_col0):
    shift = _log2(rows_per_slab)
    heads = [kvh * B_GROUP + 2 * p + par for p in range(KV_SLABS)]
    slab = lax.broadcasted_iota(jnp.int32, dist.shape, 0) >> shift
    bias = jnp.where(valid, -_per_slab(slab, [_alibi_slope(h) * LOG2E for h in heads]) * dist.astype(F32), NEG)
    if sink_in_col0:
        col = lax.broadcasted_iota(jnp.int32, dist.shape, 1)
        bias = jnp.where(col == 0, _per_slab(slab, [sinks_ref[h] * LOG2E for h in heads]), bias)
    return bias


ROW_TILE = 128


def _attn_scores(q4, keys):
    return [[_dot_nt(q4, ks[par]) for ks in keys] for par in range(2)]


def _attn_probs(scores, bias_fns):
    rows = scores[0][0].shape[0]
    tile = min(ROW_TILE, rows)
    out = []
    for par in range(2):
        probs = [[] for _ in bias_fns]
        for r in range(0, rows, tile):
            rs = slice(r, r + tile)
            st = [s[rs] + bf(par, rs) for s, bf in zip(scores[par], bias_fns)]
            m = None
            for s in st:
                full = [s[:, c:c + LANES] for c in range(0, s.shape[1] - LANES + 1, LANES)]
                mx = jnp.max(functools.reduce(jnp.maximum, full) if full else s, axis=-1, keepdims=True)
                m = mx if m is None else jnp.maximum(m, mx)
            for i, s in enumerate(st):
                probs[i].append(jnp.exp2(s - m).astype(BF16))
        out.append([jnp.concatenate(p, axis=0) for p in probs])
    return out


def _attn_out(probs, values):
    o = None
    for par in range(2):
        acc = None
        for p, vs in zip(probs[par], values):
            part = _dot(p, jnp.concatenate([vs[par], jnp.ones_like(vs[par])], axis=1))
            acc = part if acc is None else acc + part
        part = acc[:, :LANES] / acc[:, LANES:]
        o = part if o is None else o + part
    return o


def _swa_prompt_kernel(sinks_ref, q_ref, kvp_ref, kvc_ref, z_ref, o_ref, bias_ref, *, nblk):
    w = WINDOW
    j = pl.program_id(1)

    @pl.when((pl.program_id(0) == 0) & (j == 0))
    def _():
        shape = (KV_SLABS * w, 2 * w)
        qi = lax.broadcasted_iota(jnp.int32, shape, 0) & (w - 1)
        kj = lax.broadcasted_iota(jnp.int32, shape, 1) - w
        dist = qi - kj
        band = (dist >= 0) & (dist < w)
        for kvh in range(B_KVHEADS):
            for par in range(2):
                bias_ref[0, kvh, par] = _bias_table(kvh, par, sinks_ref, w, dist, band & (kj >= 0), True)
                bias_ref[1, kvh, par] = _bias_table(kvh, par, sinks_ref, w, dist, band, True)

    def halves(kv):
        return _lane_halves(kv[:, :B_KVW]), _lane_halves(kv[:, B_KVW:])

    prev = halves(kvp_ref[0])
    for i in range(nblk):
        rs = slice(i * w, (i + 1) * w)
        cur = halves(kvc_ref[0, rs, :])
        sel = jnp.minimum(j, 1) if i == 0 else 1
        kvs = [[[jnp.concatenate([_sink_slot(prev[x][kvh][par], par), cur[x][kvh][par]], axis=0)
                 for par in range(2)] for x in range(2)] for kvh in range(B_KVHEADS)]
        scores = [_attn_scores(q_ref[kvh * KV_SLABS:(kvh + 1) * KV_SLABS, rs, :].reshape(KV_SLABS * w, LANES),
                               [kvs[kvh][0]]) for kvh in range(B_KVHEADS)]
        probs = [_attn_probs(scores[kvh], [lambda par, rows, kvh=kvh, sel=sel: bias_ref[sel, kvh, par, rows, :]])
                 for kvh in range(B_KVHEADS)]
        for kvh in range(B_KVHEADS):
            o = _attn_out(probs[kvh], [kvs[kvh][1]])
            for p in range(KV_SLABS):
                cs = slice((kvh * KV_SLABS + p) * LANES, (kvh * KV_SLABS + p + 1) * LANES)
                o_ref[0, rs, cs] = (o[p * w:(p + 1) * w] * z_ref[0, rs, cs].astype(F32)).astype(BF16)
        prev = cur


def _swa_prompt(qb, kv, zb, sinks, *, nblk):
    b, t, _ = zb.shape
    tq = nblk * WINDOW
    assert t % tq == 0
    nq = t // tq
    return pl.pallas_call(
        functools.partial(_swa_prompt_kernel, nblk=nblk),
        grid=(b, nq),
        in_specs=[pl.BlockSpec(memory_space=pltpu.SMEM),
                  pl.BlockSpec((Q_SLABS, tq, LANES), lambda i, j: (0, i * nq + j, 0)),
                  pl.BlockSpec((1, WINDOW, 2 * B_KVW), lambda i, j: (i, jnp.maximum(j * nblk - 1, 0), 0)),
                  pl.BlockSpec((1, tq, 2 * B_KVW), lambda i, j: (i, j, 0)),
                  pl.BlockSpec((1, tq, B_QW), lambda i, j: (i, j, 0))],
        out_specs=pl.BlockSpec((1, tq, B_QW), lambda i, j: (i, j, 0)),
        out_shape=jax.ShapeDtypeStruct((b, t, B_QW), BF16),
        scratch_shapes=[pltpu.VMEM((2, B_KVHEADS, 2, KV_SLABS * WINDOW, 2 * WINDOW), F32)],
        compiler_params=pltpu.CompilerParams(dimension_semantics=("arbitrary", "arbitrary"),
                                             vmem_limit_bytes=VMEM_LIMIT),
        name="swa_prompt",
    )(sinks, qb, kv, kv, zb)


def _swa_sample_kernel(sinks_ref, q_ref, kvn_ref, ck_ref, cv_ref, z_ref, o_ref, cko_ref, cvo_ref,
                       biasc_ref, biasn_ref, *, t_new, seqs):
    w = ck_ref.shape[1]
    tp = kvn_ref.shape[1]
    assert 0 < t_new < WINDOW <= w

    @pl.when(pl.program_id(0) == 0)
    def _():
        t_c = lax.broadcasted_iota(jnp.int32, (KV_SLABS * tp, w), 0) & (tp - 1)
        d_c = t_c - (lax.broadcasted_iota(jnp.int32, (KV_SLABS * tp, w), 1) - w)
        t_n = lax.broadcasted_iota(jnp.int32, (KV_SLABS * tp, tp), 0) & (tp - 1)
        s_n = lax.broadcasted_iota(jnp.int32, (KV_SLABS * tp, tp), 1)
        d_n = t_n - s_n
        for kvh in range(B_KVHEADS):
            for par in range(2):
                biasc_ref[kvh, par] = _bias_table(kvh, par, sinks_ref, tp, d_c, (d_c >= 0) & (d_c < WINDOW), True)
                biasn_ref[kvh, par] = _bias_table(kvh, par, sinks_ref, tp, d_n,
                                                  (d_n >= 0) & (d_n < WINDOW) & (s_n < t_new), False)

    keys, values = [], []
    for b in range(seqs):
        ck, cv = ck_ref[b], cv_ref[b]
        kn, vn = kvn_ref[b, :, :B_KVW], kvn_ref[b, :, B_KVW:]
        cko_ref[b, 0:w - t_new, :] = ck[t_new:w, :]
        cko_ref[b, w - t_new:w, :] = kn[0:t_new, :]
        cvo_ref[b, 0:w - t_new, :] = cv[t_new:w, :]
        cvo_ref[b, w - t_new:w, :] = vn[0:t_new, :]
        kch, vch = [[[_sink_slot(t, par) for par, t in enumerate(pair)] for pair in _lane_halves(x)]
                    for x in (ck, cv)]
        knh, vnh = _lane_halves(kn), _lane_halves(vn)
        keys.append([[kch[kvh], knh[kvh]] for kvh in range(B_KVHEADS)])
        values.append([[vch[kvh], vnh[kvh]] for kvh in range(B_KVHEADS)])
    units = [(b, kvh) for b in range(seqs) for kvh in range(B_KVHEADS)]
    scores = [_attn_scores(q_ref[kvh * KV_SLABS:(kvh + 1) * KV_SLABS, b * tp:(b + 1) * tp, :]
                           .reshape(KV_SLABS * tp, LANES), keys[b][kvh]) for b, kvh in units]
    probs = [_attn_probs(s, [lambda par, rows, kvh=kvh: biasc_ref[kvh, par, rows, :],
                             lambda par, rows, kvh=kvh: biasn_ref[kvh, par, rows, :]])
             for s, (b, kvh) in zip(scores, units)]
    outs = [_attn_out(p, values[b][kvh]) for p, (b, kvh) in zip(probs, units)]
    for o, (b, kvh) in zip(outs, units):
        for p in range(KV_SLABS):
            cs = slice((kvh * KV_SLABS + p) * LANES, (kvh * KV_SLABS + p + 1) * LANES)
            o_ref[b, :, cs] = (o[p * tp:(p + 1) * tp] * z_ref[b, :, cs].astype(F32)).astype(BF16)


def _swa_sample(qb, kvn, zb, sinks, cache_k, cache_v, *, t_new, seqs):
    b, tp, _ = zb.shape
    w = cache_k.shape[1]
    assert b % seqs == 0
    blk = lambda r, c: pl.BlockSpec((seqs, r, c), lambda i: (i, 0, 0))
    tbl = lambda s: pltpu.VMEM((B_KVHEADS, 2, KV_SLABS * tp, s), F32)
    return pl.pallas_call(
        functools.partial(_swa_sample_kernel, t_new=t_new, seqs=seqs),
        grid=(b // seqs,),
        in_specs=[pl.BlockSpec(memory_space=pltpu.SMEM),
                  pl.BlockSpec((Q_SLABS, seqs * tp, LANES), lambda i: (0, i, 0)),
                  blk(tp, 2 * B_KVW), blk(w, B_KVW), blk(w, B_KVW), blk(tp, B_QW)],
        out_specs=(blk(tp, B_QW), blk(w, B_KVW), blk(w, B_KVW)),
        out_shape=(jax.ShapeDtypeStruct((b, tp, B_QW), BF16),
                   jax.ShapeDtypeStruct((b, w, B_KVW), F32),
                   jax.ShapeDtypeStruct((b, w, B_KVW), F32)),
        scratch_shapes=[tbl(w), tbl(tp)],
        compiler_params=pltpu.CompilerParams(dimension_semantics=("arbitrary",), vmem_limit_bytes=VMEM_LIMIT),
        name="swa_sample",
    )(sinks, qb, kvn, cache_k, cache_v, zb)


def _outproj_kernel(x_ref, ha_ref, hb_ref, gg_ref, wpa_ref, wpb_ref, wo_ref, gf_ref, y_ref):
    pa = _dot(ha_ref[...], wpa_ref[...])
    pb = _dot(hb_ref[...], wpb_ref[...])
    merged = gg_ref[:, :D_MODEL].astype(F32) * pa + gg_ref[:, D_MODEL:].astype(F32) * pb
    y = x_ref[...] + _dot(merged.astype(BF16), wo_ref[...])
    ms = jnp.mean(y * y, axis=-1, keepdims=True)
    y_ref[...] = y * lax.rsqrt(ms + EPS) * gf_ref[...]


def _out_projection(x2d, ha, hb, gg, w_pa, w_pb, w_out, norm_final, tm):
    m = x2d.shape[0]
    assert m % tm == 0
    row = lambda w: pl.BlockSpec((tm, w), lambda i: (i, 0))
    const = lambda shape: pl.BlockSpec(shape, lambda i: (0, 0), pipeline_mode=pl.Buffered(1))
    return pl.pallas_call(
        _outproj_kernel,
        grid=(m // tm,),
        in_specs=[row(D_MODEL), row(A_VW), row(B_QW), row(2 * D_MODEL),
                  const((A_VW, D_MODEL)), const((B_QW, D_MODEL)), const((D_MODEL, D_MODEL)), const((1, D_MODEL))],
        out_specs=row(D_MODEL),
        out_shape=jax.ShapeDtypeStruct((m, D_MODEL), F32),
        compiler_params=pltpu.CompilerParams(dimension_semantics=("arbitrary",), vmem_limit_bytes=VMEM_LIMIT),
        name="out_projection",
    )(x2d, ha, hb, gg, w_pa, w_pb, w_out, norm_final.reshape(1, D_MODEL))


def _layer(x, weights, *, hgrn_state, cache, t_valid):
    norm_in, w_in, lb_logits, a_gnorm, b_sinks, w_pa, w_pb, w_out, norm_final = weights
    b, t, d = x.shape
    x2d = x.reshape(b * t, d)
    q, k, lf, v, za, qb, kv, zb, gg = _in_projection(x2d, norm_in, lb_logits, a_gnorm, w_in, min(TM_IN, b * t))
    r3 = lambda a: a.reshape(b, t, a.shape[-1])
    kv3 = r3(kv)
    if cache is None:
        ha, s_new = _hgrn_prompt(r3(q), r3(k), r3(v), r3(lf), r3(za),
                                 tb=min(512, t), chunk=HGRN_CHUNK, sub=HGRN_SUB)
        hb = _swa_prompt(qb, kv3, r3(zb), b_sinks, nblk=min(4, t // WINDOW))
        last = min(WINDOW, t)
        ck = kv3[:, t - last:, :B_KVW]
        cv = kv3[:, t - last:, B_KVW:]
    else:
        ha, s_new = _hgrn_sample(r3(q), r3(k), r3(v), r3(lf), r3(za), hgrn_state,
                                 t_valid=t_valid, seqs=min(4, b))
        hb, ck, cv = _swa_sample(qb, kv3, r3(zb), b_sinks, cache[0], cache[1], t_new=t_valid, seqs=min(8, b))
    y = _out_projection(x2d, ha.reshape(b * t, A_VW), hb.reshape(b * t, B_QW), gg, w_pa, w_pb, w_out, norm_final,
                        min(TM_OUT, b * t))
    return y.reshape(b, t, d), s_new, ck, cv


def kernel(x_prompt, x_sample, state_hgrn, cache_k, cache_v, norm_in, w_in, lb_logits, a_gnorm, b_sinks,
           w_pa, w_pb, w_out, norm_final):
    assert state_hgrn.shape[0] == 1, "single decoder layer"
    weights = (norm_in[0], w_in[0].astype(BF16), lb_logits, a_gnorm[0], b_sinks[0],
               w_pa[0].astype(BF16), w_pb[0].astype(BF16), w_out[0].astype(BF16), norm_final)
    y_p, s_p, ck_p, cv_p = _layer(x_prompt, weights, hgrn_state=None, cache=None, t_valid=None)
    bs, ts, _ = x_sample.shape
    w = cache_k.shape[2]
    xs = jnp.pad(x_sample, ((0, 0), (0, SAMPLE_PAD - ts), (0, 0)))
    cache = (cache_k[0].reshape(bs, w, B_KVW), cache_v[0].reshape(bs, w, B_KVW))
    y_s, s_s, ck_s, cv_s = _layer(xs, weights, hgrn_state=state_hgrn[0], cache=cache, t_valid=ts)
    kvshape = lambda a: a.reshape(1, a.shape[0], a.shape[1], B_KVHEADS, B_HDIM)
    return (y_p, y_s[:, :ts], s_p[None], kvshape(ck_p), kvshape(cv_p),
            s_s[None], kvshape(ck_s), kvshape(cv_s))
```

```python
import functools
import math

import jax
import jax.numpy as jnp
from jax import lax
from jax.experimental import pallas as pl
from jax.experimental.pallas import tpu as pltpu

F32 = jnp.float32
BF16 = jnp.bfloat16

D_MODEL = 1024
A_HEADS = 8
A_KDIM = 128
A_VDIM = D_MODEL // A_HEADS
A_KW = A_HEADS * A_KDIM
A_VW = A_HEADS * A_VDIM
B_QHEADS = 16
B_KVHEADS = 2
B_HDIM = 64
B_GROUP = B_QHEADS // B_KVHEADS
B_QW = B_QHEADS * B_HDIM
B_KVW = B_KVHEADS * B_HDIM
WINDOW = 128
EPS = 1e-6
NEG = -1e30
LANES = 128
SAMPLE_PAD = 16
Q_SLABS = B_QW // LANES
KV_SLABS = B_GROUP * B_HDIM // LANES

_SPLITS = (A_KW, A_KW, A_VW, A_VW, B_QW, B_KVW, B_KVW, B_QW, D_MODEL, D_MODEL)
_OFFS = tuple(int(sum(_SPLITS[:i])) for i in range(len(_SPLITS)))
IN_COLS = int(sum(_SPLITS))

VMEM_LIMIT = 56 * 1024 * 1024

TM_IN = 256
TM_OUT = 512
PROJ_COLS = 256
HGRN_CHUNK = 64
HGRN_SUB = 16


def _sigmoid(x):
    return 1.0 / (1.0 + jnp.exp(-x))


def _silu(x):
    return x * _sigmoid(x)


def _dot_nt(a, b):
    return lax.dot_general(a, b, (((1,), (1,)), ((), ())), preferred_element_type=F32)


def _dot_tn(a, b, precision=None):
    return lax.dot_general(a, b, (((0,), (0,)), ((), ())), precision=precision, preferred_element_type=F32)


def _dot(a, b):
    return jnp.dot(a, b, preferred_element_type=F32)


def _log2(n):
    assert n & (n - 1) == 0
    return n.bit_length() - 1


def _block_cumsum(x, sub):
    rows = lax.broadcasted_iota(jnp.int32, x.shape, 0) & (sub - 1)
    shift = 1
    while shift < sub:
        x = x + jnp.where(rows >= shift, pltpu.roll(x, shift, axis=0), 0.0)
        shift *= 2
    return x


def _block_row(x, r, sub):
    return jnp.concatenate([jnp.broadcast_to(x[i + r:i + r + 1], (sub, x.shape[1]))
                            for i in range(0, x.shape[0], sub)], axis=0)


def _inproj_kernel(x_ref, g_ref, lbl_ref, gn_ref, w_ref,
                   qd_ref, kd_ref, qt_ref, kt_ref, tau_ref, v_ref, za_ref, qb_ref, kv_ref, zb_ref, gg_ref, ell_ref,
                   *, t_valid):
    sub = HGRN_SUB
    x = x_ref[...]
    ms = jnp.mean(x * x, axis=-1, keepdims=True)
    xn = (x * lax.rsqrt(ms + EPS) * g_ref[...]).astype(BF16)

    def seg(i, width=None):
        width = _SPLITS[i] if width is None else width
        return _dot(xn, w_ref[:, _OFFS[i]:_OFFS[i] + width])

    l = lbl_ref[...]
    e = jnp.exp(l - jnp.max(l, axis=0, keepdims=True))
    lb = e[0:1, :] / jnp.sum(e, axis=0, keepdims=True)

    qa = seg(0)
    fa = seg(1)
    n = tau_ref.shape[1]

    def hgrn_head(h):
        cs = slice(h * A_KDIM, (h + 1) * A_KDIM)
        q = _silu(qa[:, cs])
        sig = _sigmoid(fa[:, cs])
        k = (1.0 - lb[:, cs]) * (1.0 - sig)
        lf = jnp.log2(lb[:, cs] + (1.0 - lb[:, cs]) * sig)
        if t_valid is not None:
            lf = jnp.where((lax.broadcasted_iota(jnp.int32, lf.shape, 0) & (sub - 1)) < t_valid, lf, 0.0)
        ell = _block_cumsum(lf, sub)
        ell_ref[h] = ell
        for c in range(tau_ref.shape[0]):
            tau_ref[c, :, cs] = ell_ref[h, pl.ds(c * n * sub + sub - 1, n, stride=sub), :]
        tau = _block_row(ell, sub - 1, sub)
        mu = _block_row(ell, sub // 2 - 1, sub)
        qt_ref[:, cs] = (q * jnp.exp2(ell)).astype(BF16)
        kt_ref[:, cs] = (k * jnp.exp2(tau - ell)).astype(BF16)
        qd_ref[:, cs] = (q * jnp.exp2(ell - mu)).astype(BF16)
        kd_ref[:, cs] = (k * jnp.exp2(mu - ell)).astype(BF16)

    def others():
        v_ref[...] = seg(2).astype(BF16)
        yield
        za_ref[...] = (_silu(seg(3)) * gn_ref[...]).astype(BF16)
        yield
        qb = (seg(4) * Q_SCALE).astype(BF16)
        for p in range(Q_SLABS):
            qb_ref[p] = qb[:, p * LANES:(p + 1) * LANES]
        yield
        kv_ref[...] = seg(5, 2 * B_KVW)
        zb_ref[...] = _silu(seg(7)).astype(BF16)
        yield
        for half in range(2):
            cs = slice(half * D_MODEL, (half + 1) * D_MODEL)
            gg_ref[:, cs] = _sigmoid(_dot(xn, w_ref[:, _OFFS[8 + half]:_OFFS[8 + half] + D_MODEL])).astype(BF16)
            yield

    rest = others()
    for h in range(A_HEADS):
        hgrn_head(h)
        next(rest, None)
    for _ in rest:
        pass


def _in_projection(x2d, norm_in, lb_logits, a_gnorm, w_in_bf16, tm, chunk, t_valid):
    m = x2d.shape[0]
    assert m % tm == 0 and tm % chunk == 0 and chunk % HGRN_SUB == 0
    row = lambda w: pl.BlockSpec((tm, w), lambda i: (i, 0))
    const = lambda shape: pl.BlockSpec(shape, lambda i: (0, 0), pipeline_mode=pl.Buffered(1))
    n_sub = chunk // HGRN_SUB
    out_shapes = (
        jax.ShapeDtypeStruct((m, A_KW), BF16),
        jax.ShapeDtypeStruct((m, A_KW), BF16),
        jax.ShapeDtypeStruct((m, A_KW), BF16),
        jax.ShapeDtypeStruct((m, A_KW), BF16),
        jax.ShapeDtypeStruct((m // chunk, n_sub, A_KW), F32),
        jax.ShapeDtypeStruct((m, A_VW), BF16),
        jax.ShapeDtypeStruct((m, A_VW), BF16),
        jax.ShapeDtypeStruct((Q_SLABS, m, LANES), BF16),
        jax.ShapeDtypeStruct((m, 2 * B_KVW), F32),
        jax.ShapeDtypeStruct((m, B_QW), BF16),
        jax.ShapeDtypeStruct((m, 2 * D_MODEL), BF16),
    )
    def spec(s):
        if len(s.shape) == 2:
            return row(s.shape[1])
        if s.shape[2] == LANES:
            return pl.BlockSpec((Q_SLABS, tm, LANES), lambda i: (0, i, 0))
        return pl.BlockSpec((tm // chunk, n_sub, A_KW), lambda i: (i, 0, 0))

    return pl.pallas_call(
        functools.partial(_inproj_kernel, t_valid=t_valid),
        grid=(m // tm,),
        in_specs=[row(D_MODEL), const((1, D_MODEL)), const(lb_logits.shape), const((1, A_VW)),
                  const((D_MODEL, IN_COLS))],
        out_specs=tuple(spec(s) for s in out_shapes),
        out_shape=out_shapes,
        scratch_shapes=[pltpu.VMEM((A_HEADS, tm, A_KDIM), F32)],
        compiler_params=pltpu.CompilerParams(dimension_semantics=("arbitrary",), vmem_limit_bytes=VMEM_LIMIT),
        name="in_projection",
    )(x2d, norm_in.reshape(1, D_MODEL), lb_logits, jnp.tile(a_gnorm, A_HEADS).reshape(1, A_VW), w_in_bf16)


def _hgrn_operands(qd, kd, qt, kt, tau, chunk, sub):
    n = chunk // sub
    beta = [jnp.zeros_like(tau[0:1])]
    for i in range(n):
        beta.append(beta[i] + tau[i:i + 1])
    total = beta[n]

    def brow(x):
        return jnp.broadcast_to(x, (sub, x.shape[1])).astype(BF16)

    blk = lambda x, i: x[i * sub:(i + 1) * sub]
    qhat = jnp.concatenate([blk(qt, i) * brow(jnp.exp2(beta[i])) for i in range(n)], axis=0)
    khat = jnp.concatenate([blk(kt, i) * brow(jnp.exp2(total - beta[i + 1])) for i in range(n)], axis=0)
    zero_blk = jnp.zeros((sub, A_KDIM), BF16)
    qts, kins = [], []
    for i in range(1, n):
        kin = [blk(kt, j) * brow(jnp.exp2(beta[i] - beta[j + 1])) for j in range(i)]
        kins.append(jnp.concatenate(kin + [zero_blk] * (n - i), axis=0))
        qts.append(blk(qt, i))
    return dict(qd=qd, kd=kd, qhat=qhat, khat=khat, qts=qts, kins=kins, total=total)


def _hgrn_scores(ops, st, masks, state_is_kv):
    causal, same = masks
    sb = st.astype(BF16)
    o_inter = _dot(ops["qhat"], sb) if state_is_kv else _dot_nt(ops["qhat"], sb)
    att = jnp.where(causal, _dot_nt(ops["qd"], ops["kd"]), 0.0)
    if ops["qts"]:
        sub, chunk = ops["qts"][0].shape[0], att.shape[0]
        off = [jnp.zeros((sub, chunk), F32)] + [_dot_nt(qt, kin) for qt, kin in zip(ops["qts"], ops["kins"])]
        att = jnp.where(same, att, jnp.concatenate(off, axis=0))
    return o_inter, att.astype(BF16)


def _hgrn_update(ops, scores, v, st, state_is_kv):
    o_inter, att = scores
    o = o_inter + _dot(att, v)
    if state_is_kv:
        total8 = jnp.broadcast_to(ops["total"], (8, A_KDIM))
        first = jnp.where(lax.broadcasted_iota(jnp.int32, (8, A_VDIM), 0) == 0, 1.0, 0.0)
        total_col = _dot_tn(total8, first, precision=lax.Precision.HIGHEST)
        st_new = st * jnp.exp2(total_col) + _dot_tn(ops["khat"], v)
    else:
        st_new = st * jnp.exp2(ops["total"]) + _dot_tn(v, ops["khat"])
    return o, st_new


def _chunk_masks(chunk, sub):
    r = lax.broadcasted_iota(jnp.int32, (chunk, chunk), 0)
    s = lax.broadcasted_iota(jnp.int32, (chunk, chunk), 1)
    return s <= r, ((r ^ s) & ~(sub - 1)) == 0


def _hgrn_finish(o, z):
    return (o * lax.rsqrt(jnp.mean(o * o, axis=-1, keepdims=True) + EPS) * z.astype(F32)).astype(BF16)


def _hgrn_prompt_kernel(qd_ref, kd_ref, qt_ref, kt_ref, tau_ref, v_ref, z_ref, o_ref, sout_ref, st_ref, *,
                        chunk, sub):
    tb = v_ref.shape[1]
    nt = pl.program_id(1)

    @pl.when(nt == 0)
    def _():
        st_ref[...] = jnp.zeros_like(st_ref)

    masks = _chunk_masks(chunk, sub)
    cols = [slice(h * A_KDIM, (h + 1) * A_KDIM) for h in range(A_HEADS)]

    def body(ci, carry):
        rs = pl.ds(pl.multiple_of(ci * chunk, chunk), chunk)
        ops = [_hgrn_operands(qd_ref[0, rs, cs], kd_ref[0, rs, cs], qt_ref[0, rs, cs], kt_ref[0, rs, cs],
                              tau_ref[0, ci, :, cs], chunk, sub) for cs in cols]
        scores = [_hgrn_scores(ops[h], st_ref[h], masks, False) for h in range(A_HEADS)]
        outs = []
        for h, cs in enumerate(cols):
            o, st_ref[h] = _hgrn_update(ops[h], scores[h], v_ref[0, rs, cs], st_ref[h], False)
            outs.append(o)
        for o, cs in zip(outs, cols):
            o_ref[0, rs, cs] = _hgrn_finish(o, z_ref[0, rs, cs])
        return carry

    lax.fori_loop(0, tb // chunk, body, 0, unroll=4)

    @pl.when(nt == pl.num_programs(1) - 1)
    def _():
        for h in range(A_HEADS):
            sout_ref[0, h] = st_ref[h].T


def _tau_spec(tb, chunk, sub):
    return pl.BlockSpec((1, tb // chunk, chunk // sub, A_KW), lambda i, j: (i, j, 0, 0))


def _hgrn_prompt(qd, kd, qt, kt, tau, v, z, *, tb, chunk, sub):
    b, t, _ = v.shape
    assert t % tb == 0 and tb % chunk == 0 and chunk % sub == 0
    blk = lambda w: pl.BlockSpec((1, tb, w), lambda i, j: (i, j, 0))
    return pl.pallas_call(
        functools.partial(_hgrn_prompt_kernel, chunk=chunk, sub=sub),
        grid=(b, t // tb),
        in_specs=[blk(A_KW), blk(A_KW), blk(A_KW), blk(A_KW), _tau_spec(tb, chunk, sub), blk(A_VW), blk(A_VW)],
        out_specs=(blk(A_VW), pl.BlockSpec((1, A_HEADS, A_KDIM, A_VDIM), lambda i, j: (i, 0, 0, 0))),
        out_shape=(jax.ShapeDtypeStruct((b, t, A_VW), BF16),
                   jax.ShapeDtypeStruct((b, A_HEADS, A_KDIM, A_VDIM), F32)),
        scratch_shapes=[pltpu.VMEM((A_HEADS, A_VDIM, A_KDIM), F32)],
        compiler_params=pltpu.CompilerParams(dimension_semantics=("arbitrary", "arbitrary"),
                                             vmem_limit_bytes=VMEM_LIMIT),
        name="hgrn2_prompt",
    )(qd, kd, qt, kt, tau, v, z)


def _hgrn_sample_kernel(qd_ref, kd_ref, qt_ref, kt_ref, tau_ref, v_ref, z_ref, s0_ref, o_ref, sout_ref, *, seqs):
    tp = v_ref.shape[1]
    masks = _chunk_masks(tp, tp)
    units = [(b, h, slice(h * A_KDIM, (h + 1) * A_KDIM)) for b in range(seqs) for h in range(A_HEADS)]
    ops = [_hgrn_operands(qd_ref[b, :, cs], kd_ref[b, :, cs], qt_ref[b, :, cs], kt_ref[b, :, cs],
                          tau_ref[b, :, cs], tp, tp) for b, h, cs in units]
    scores = [_hgrn_scores(op, s0_ref[b, h], masks, True) for op, (b, h, cs) in zip(ops, units)]
    outs = []
    for op, sc, (b, h, cs) in zip(ops, scores, units):
        o, sout_ref[b, h] = _hgrn_update(op, sc, v_ref[b, :, cs], s0_ref[b, h], True)
        outs.append(o)
    for o, (b, h, cs) in zip(outs, units):
        o_ref[b, :, cs] = _hgrn_finish(o, z_ref[b, :, cs])


def _hgrn_sample(qd, kd, qt, kt, tau, v, z, state, *, seqs):
    b, tp, _ = v.shape
    assert b % seqs == 0 and tau.shape == (b, 1, A_KW)
    blk = lambda w: pl.BlockSpec((seqs, tp, w), lambda i: (i, 0, 0))
    sblk = pl.BlockSpec((seqs, A_HEADS, A_KDIM, A_VDIM), lambda i: (i, 0, 0, 0))
    return pl.pallas_call(
        functools.partial(_hgrn_sample_kernel, seqs=seqs),
        grid=(b // seqs,),
        in_specs=[blk(A_KW), blk(A_KW), blk(A_KW), blk(A_KW), pl.BlockSpec((seqs, 1, A_KW), lambda i: (i, 0, 0)),
                  blk(A_VW), blk(A_VW), sblk],
        out_specs=(blk(A_VW), sblk),
        out_shape=(jax.ShapeDtypeStruct((b, tp, A_VW), BF16),
                   jax.ShapeDtypeStruct((b, A_HEADS, A_KDIM, A_VDIM), F32)),
        compiler_params=pltpu.CompilerParams(dimension_semantics=("arbitrary",), vmem_limit_bytes=VMEM_LIMIT),
        name="hgrn2_sample",
    )(qd, kd, qt, kt, tau, v, z, state)


LOG2E = math.log2(math.e)
Q_SCALE = LOG2E / math.sqrt(B_HDIM)


def _alibi_slope(h):
    return 2.0 ** (-8.0 * (h + 1) / B_QHEADS)


def _lane_halves(x):
    low = lax.broadcasted_iota(jnp.int32, x.shape, 1) < B_HDIM
    swapped = pltpu.roll(x, B_HDIM, axis=1)
    both = (jnp.where(low, x, swapped), jnp.where(low, swapped, x))
    return [[jnp.where(low, h, 0.0).astype(BF16), jnp.where(low, 0.0, h).astype(BF16)] for h in both]


def _per_slab(slab, values):
    out = jnp.full(slab.shape, values[0], F32)
    for p in range(1, len(values)):
        out = jnp.where(slab == p, values[p], out)
    return out


def _sink_slot(tile, par):
    head = tile[:16]
    row0 = lax.broadcasted_iota(jnp.int32, head.shape, 0) == 0
    low = lax.broadcasted_iota(jnp.int32, head.shape, 1) < B_HDIM
    feat = low if par == 0 else jnp.logical_not(low)
    return jnp.concatenate([jnp.where(row0 & feat, jnp.zeros_like(head), head), tile[16:]], axis=0)


def _bias_table(kvh, par, sinks_ref, rows_per_slab, dist, valid, sink_in_col0):
    shift = _log2(rows_per_slab)
    heads = [kvh * B_GROUP + 2 * p + par for p in range(KV_SLABS)]
    slab = lax.broadcasted_iota(jnp.int32, dist.shape, 0) >> shift
    bias = jnp.where(valid, -_per_slab(slab, [_alibi_slope(h) * LOG2E for h in heads]) * dist.astype(F32), NEG)
    if sink_in_col0:
        col = lax.broadcasted_iota(jnp.int32, dist.shape, 1)
        bias = jnp.where(col == 0, _per_slab(slab, [sinks_ref[h] * LOG2E for h in heads]), bias)
    return bias


ROW_TILE = 128


def _attn_scores(q4, keys):
    return [[_dot_nt(q4, ks[par]) for ks in keys] for par in range(2)]


def _attn_probs(scores, bias_fns):
    rows = scores[0][0].shape[0]
    tile = min(ROW_TILE, rows)
    out = []
    for par in range(2):
        probs = [[] for _ in bias_fns]
        for r in range(0, rows, tile):
            rs = slice(r, r + tile)
            st = [s[rs] + bf(par, rs) for s, bf in zip(scores[par], bias_fns)]
            m = None
            for s in st:
                full = [s[:, c:c + LANES] for c in range(0, s.shape[1] - LANES + 1, LANES)]
                mx = jnp.max(functools.reduce(jnp.maximum, full) if full else s, axis=-1, keepdims=True)
                m = mx if m is None else jnp.maximum(m, mx)
            for i, s in enumerate(st):
                probs[i].append(jnp.exp2(s - m).astype(BF16))
        out.append([jnp.concatenate(p, axis=0) for p in probs])
    return out


def _attn_out(probs, values):
    o = None
    for par in range(2):
        acc = None
        for p, vs in zip(probs[par], values):
            part = _dot(p, jnp.concatenate([vs[par], jnp.ones_like(vs[par])], axis=1))
            acc = part if acc is None else acc + part
        part = acc[:, :LANES] / acc[:, LANES:]
        o = part if o is None else o + part
    return o


def _swa_prompt_kernel(sinks_ref, q_ref, kvp_ref, kvc_ref, z_ref, o_ref, bias_ref, *, nblk):
    w = WINDOW
    j = pl.program_id(1)

    @pl.when((pl.program_id(0) == 0) & (j == 0))
    def _():
        shape = (KV_SLABS * w, 2 * w)
        qi = lax.broadcasted_iota(jnp.int32, shape, 0) & (w - 1)
        kj = lax.broadcasted_iota(jnp.int32, shape, 1) - w
        dist = qi - kj
        band = (dist >= 0) & (dist < w)
        for kvh in range(B_KVHEADS):
            for par in range(2):
                bias_ref[0, kvh, par] = _bias_table(kvh, par, sinks_ref, w, dist, band & (kj >= 0), True)
                bias_ref[1, kvh, par] = _bias_table(kvh, par, sinks_ref, w, dist, band, True)

    def halves(kv):
        return _lane_halves(kv[:, :B_KVW]), _lane_halves(kv[:, B_KVW:])

    prev = halves(kvp_ref[0])
    for i in range(nblk):
        rs = slice(i * w, (i + 1) * w)
        cur = halves(kvc_ref[0, rs, :])
        sel = jnp.minimum(j, 1) if i == 0 else 1
        kvs = [[[jnp.concatenate([_sink_slot(prev[x][kvh][par], par), cur[x][kvh][par]], axis=0)
                 for par in range(2)] for x in range(2)] for kvh in range(B_KVHEADS)]
        scores = [_attn_scores(q_ref[kvh * KV_SLABS:(kvh + 1) * KV_SLABS, rs, :].reshape(KV_SLABS * w, LANES),
                               [kvs[kvh][0]]) for kvh in range(B_KVHEADS)]
        probs = [_attn_probs(scores[kvh], [lambda par, rows, kvh=kvh, sel=sel: bias_ref[sel, kvh, par, rows, :]])
                 for kvh in range(B_KVHEADS)]
        for kvh in range(B_KVHEADS):
            o = _attn_out(probs[kvh], [kvs[kvh][1]])
            for p in range(KV_SLABS):
                cs = slice((kvh * KV_SLABS + p) * LANES, (kvh * KV_SLABS + p + 1) * LANES)
                o_ref[0, rs, cs] = (o[p * w:(p + 1) * w] * z_ref[0, rs, cs].astype(F32)).astype(BF16)
        prev = cur


def _swa_prompt(qb, kv, zb, sinks, *, nblk):
    b, t, _ = zb.shape
    tq = nblk * WINDOW
    assert t % tq == 0
    nq = t // tq
    return pl.pallas_call(
        functools.partial(_swa_prompt_kernel, nblk=nblk),
        grid=(b, nq),
        in_specs=[pl.BlockSpec(memory_space=pltpu.SMEM),
                  pl.BlockSpec((Q_SLABS, tq, LANES), lambda i, j: (0, i * nq + j, 0)),
                  pl.BlockSpec((1, WINDOW, 2 * B_KVW), lambda i, j: (i, jnp.maximum(j * nblk - 1, 0), 0)),
                  pl.BlockSpec((1, tq, 2 * B_KVW), lambda i, j: (i, j, 0)),
                  pl.BlockSpec((1, tq, B_QW), lambda i, j: (i, j, 0))],
        out_specs=pl.BlockSpec((1, tq, B_QW), lambda i, j: (i, j, 0)),
        out_shape=jax.ShapeDtypeStruct((b, t, B_QW), BF16),
        scratch_shapes=[pltpu.VMEM((2, B_KVHEADS, 2, KV_SLABS * WINDOW, 2 * WINDOW), F32)],
        compiler_params=pltpu.CompilerParams(dimension_semantics=("arbitrary", "arbitrary"),
                                             vmem_limit_bytes=VMEM_LIMIT),
        name="swa_prompt",
    )(sinks, qb, kv, kv, zb)


def _swa_sample_kernel(sinks_ref, q_ref, kvn_ref, ck_ref, cv_ref, z_ref, o_ref, cko_ref, cvo_ref,
                       biasc_ref, biasn_ref, *, t_new, seqs):
    w = ck_ref.shape[1]
    tp = kvn_ref.shape[1]
    assert 0 < t_new < WINDOW <= w

    @pl.when(pl.program_id(0) == 0)
    def _():
        t_c = lax.broadcasted_iota(jnp.int32, (KV_SLABS * tp, w), 0) & (tp - 1)
        d_c = t_c - (lax.broadcasted_iota(jnp.int32, (KV_SLABS * tp, w), 1) - w)
        t_n = lax.broadcasted_iota(jnp.int32, (KV_SLABS * tp, tp), 0) & (tp - 1)
        s_n = lax.broadcasted_iota(jnp.int32, (KV_SLABS * tp, tp), 1)
        d_n = t_n - s_n
        for kvh in range(B_KVHEADS):
            for par in range(2):
                biasc_ref[kvh, par] = _bias_table(kvh, par, sinks_ref, tp, d_c, (d_c >= 0) & (d_c < WINDOW), True)
                biasn_ref[kvh, par] = _bias_table(kvh, par, sinks_ref, tp, d_n,
                                                  (d_n >= 0) & (d_n < WINDOW) & (s_n < t_new), False)

    keys, values = [], []
    for b in range(seqs):
        ck, cv = ck_ref[b], cv_ref[b]
        kn, vn = kvn_ref[b, :, :B_KVW], kvn_ref[b, :, B_KVW:]
        cko_ref[b, 0:w - t_new, :] = ck[t_new:w, :]
        cko_ref[b, w - t_new:w, :] = kn[0:t_new, :]
        cvo_ref[b, 0:w - t_new, :] = cv[t_new:w, :]
        cvo_ref[b, w - t_new:w, :] = vn[0:t_new, :]
        kch, vch = [[[_sink_slot(t, par) for par, t in enumerate(pair)] for pair in _lane_halves(x)]
                    for x in (ck, cv)]
        knh, vnh = _lane_halves(kn), _lane_halves(vn)
        keys.append([[kch[kvh], knh[kvh]] for kvh in range(B_KVHEADS)])
        values.append([[vch[kvh], vnh[kvh]] for kvh in range(B_KVHEADS)])
    units = [(b, kvh) for b in range(seqs) for kvh in range(B_KVHEADS)]
    scores = [_attn_scores(q_ref[kvh * KV_SLABS:(kvh + 1) * KV_SLABS, b * tp:(b + 1) * tp, :]
                           .reshape(KV_SLABS * tp, LANES), keys[b][kvh]) for b, kvh in units]
    probs = [_attn_probs(s, [lambda par, rows, kvh=kvh: biasc_ref[kvh, par, rows, :],
                             lambda par, rows, kvh=kvh: biasn_ref[kvh, par, rows, :]])
             for s, (b, kvh) in zip(scores, units)]
    outs = [_attn_out(p, values[b][kvh]) for p, (b, kvh) in zip(probs, units)]
    for o, (b, kvh) in zip(outs, units):
        for p in range(KV_SLABS):
            cs = slice((kvh * KV_SLABS + p) * LANES, (kvh * KV_SLABS + p + 1) * LANES)
            o_ref[b, :, cs] = (o[p * tp:(p + 1) * tp] * z_ref[b, :, cs].astype(F32)).astype(BF16)


def _swa_sample(qb, kvn, zb, sinks, cache_k, cache_v, *, t_new, seqs):
    b, tp, _ = zb.shape
    w = cache_k.shape[1]
    assert b % seqs == 0
    blk = lambda r, c: pl.BlockSpec((seqs, r, c), lambda i: (i, 0, 0))
    tbl = lambda s: pltpu.VMEM((B_KVHEADS, 2, KV_SLABS * tp, s), F32)
    return pl.pallas_call(
        functools.partial(_swa_sample_kernel, t_new=t_new, seqs=seqs),
        grid=(b // seqs,),
        in_specs=[pl.BlockSpec(memory_space=pltpu.SMEM),
                  pl.BlockSpec((Q_SLABS, seqs * tp, LANES), lambda i: (0, i, 0)),
                  blk(tp, 2 * B_KVW), blk(w, B_KVW), blk(w, B_KVW), blk(tp, B_QW)],
        out_specs=(blk(tp, B_QW), blk(w, B_KVW), blk(w, B_KVW)),
        out_shape=(jax.ShapeDtypeStruct((b, tp, B_QW), BF16),
                   jax.ShapeDtypeStruct((b, w, B_KVW), F32),
                   jax.ShapeDtypeStruct((b, w, B_KVW), F32)),
        scratch_shapes=[tbl(w), tbl(tp)],
        compiler_params=pltpu.CompilerParams(dimension_semantics=("arbitrary",), vmem_limit_bytes=VMEM_LIMIT),
        name="swa_sample",
    )(sinks, qb, kvn, cache_k, cache_v, zb)


def _merge_project(x, ha, hb, gg, wpa_ref, wpb_ref, wo_ref, gf):
    pa = _dot(ha, wpa_ref[...])
    pb = _dot(hb, wpb_ref[...])
    merged = gg[:, :D_MODEL].astype(F32) * pa + gg[:, D_MODEL:].astype(F32) * pb
    y = x + _dot(merged.astype(BF16), wo_ref[...])
    return y * lax.rsqrt(jnp.mean(y * y, axis=-1, keepdims=True) + EPS) * gf


def _outproj_kernel(x_ref, ha_ref, hb_ref, gg_ref, wpa_ref, wpb_ref, wo_ref, gf_ref, y_ref):
    y_ref[...] = _merge_project(x_ref[...], ha_ref[...], hb_ref[...], gg_ref[...], wpa_ref, wpb_ref, wo_ref,
                                gf_ref[...])


def _out_projection(x2d, ha, hb, gg, w_pa, w_pb, w_out, norm_final, tm):
    m = x2d.shape[0]
    assert m % tm == 0
    row = lambda w: pl.BlockSpec((tm, w), lambda i: (i, 0))
    const = lambda shape: pl.BlockSpec(shape, lambda i: (0, 0), pipeline_mode=pl.Buffered(1))
    return pl.pallas_call(
        _outproj_kernel,
        grid=(m // tm,),
        in_specs=[row(D_MODEL), row(A_VW), row(B_QW), row(2 * D_MODEL),
                  const((A_VW, D_MODEL)), const((B_QW, D_MODEL)), const((D_MODEL, D_MODEL)), const((1, D_MODEL))],
        out_specs=row(D_MODEL),
        out_shape=jax.ShapeDtypeStruct((m, D_MODEL), F32),
        compiler_params=pltpu.CompilerParams(dimension_semantics=("arbitrary",), vmem_limit_bytes=VMEM_LIMIT),
        name="out_projection",
    )(x2d, ha, hb, gg, w_pa, w_pb, w_out, norm_final.reshape(1, D_MODEL))


def _layer(x, weights, *, hgrn_state, cache, t_valid):
    norm_in, w_in, lb_logits, a_gnorm, b_sinks, w_pa, w_pb, w_out, norm_final = weights
    b, t, d = x.shape
    x2d = x.reshape(b * t, d)
    chunk = HGRN_CHUNK if cache is None else t
    qd, kd, qt, kt, tau, v, za, qb, kv, zb, gg = _in_projection(x2d, norm_in, lb_logits, a_gnorm, w_in,
                                                                min(TM_IN, b * t), chunk, t_valid)
    r3 = lambda a: a.reshape(b, t, a.shape[-1])
    kv3 = r3(kv)
    tau = tau.reshape(b, t // chunk, chunk // HGRN_SUB, A_KW)
    if cache is None:
        ha, s_new = _hgrn_prompt(r3(qd), r3(kd), r3(qt), r3(kt), tau, r3(v), r3(za),
                                 tb=min(512, t), chunk=chunk, sub=HGRN_SUB)
        hb = _swa_prompt(qb, kv3, r3(zb), b_sinks, nblk=min(4, t // WINDOW))
        last = min(WINDOW, t)
        ck = kv3[:, t - last:, :B_KVW]
        cv = kv3[:, t - last:, B_KVW:]
    else:
        assert t == HGRN_SUB
        ha, s_new = _hgrn_sample(r3(qd), r3(kd), r3(qt), r3(kt), tau.reshape(b, 1, A_KW), r3(v), r3(za), hgrn_state,
                                 seqs=min(4, b))
        hb, ck, cv = _swa_sample(qb, kv3, r3(zb), b_sinks, cache[0], cache[1], t_new=t_valid, seqs=min(8, b))
    y = _out_projection(x2d, ha.reshape(b * t, A_VW), hb.reshape(b * t, B_QW), gg, w_pa, w_pb, w_out, norm_final,
                        min(TM_OUT, b * t))
    return y.reshape(b, t, d), s_new, ck, cv


def kernel(x_prompt, x_sample, state_hgrn, cache_k, cache_v, norm_in, w_in, lb_logits, a_gnorm, b_sinks,
           w_pa, w_pb, w_out, norm_final):
    assert state_hgrn.shape[0] == 1, "single decoder layer"
    weights = (norm_in[0], w_in[0].astype(BF16), lb_logits, a_gnorm[0], b_sinks[0],
               w_pa[0].astype(BF16), w_pb[0].astype(BF16), w_out[0].astype(BF16), norm_final)
    y_p, s_p, ck_p, cv_p = _layer(x_prompt, weights, hgrn_state=None, cache=None, t_valid=None)
    bs, ts, _ = x_sample.shape
    w = cache_k.shape[2]
    xs = jnp.pad(x_sample, ((0, 0), (0, SAMPLE_PAD - ts), (0, 0)))
    cache = (cache_k[0].reshape(bs, w, B_KVW), cache_v[0].reshape(bs, w, B_KVW))
    y_s, s_s, ck_s, cv_s = _layer(xs, weights, hgrn_state=state_hgrn[0], cache=cache, t_valid=ts)
    kvshape = lambda a: a.reshape(1, a.shape[0], a.shape[1], B_KVHEADS, B_HDIM)
    return (y_p, y_s[:, :ts], s_p[None], kvshape(ck_p), kvshape(cv_p),
            s_s[None], kvshape(ck_s), kvshape(cv_s))
```

```python
import functools
import math

import jax
import jax.numpy as jnp
from jax import lax
from jax.experimental import pallas as pl
from jax.experimental.pallas import tpu as pltpu

F32 = jnp.float32
BF16 = jnp.bfloat16

D_MODEL = 1024
A_HEADS = 8
A_KDIM = 128
A_VDIM = D_MODEL // A_HEADS
A_KW = A_HEADS * A_KDIM
A_VW = A_HEADS * A_VDIM
B_QHEADS = 16
B_KVHEADS = 2
B_HDIM = 64
B_GROUP = B_QHEADS // B_KVHEADS
B_QW = B_QHEADS * B_HDIM
B_KVW = B_KVHEADS * B_HDIM
WINDOW = 128
EPS = 1e-6
NEG = -1e30
LANES = 128
SAMPLE_PAD = 16
Q_SLABS = B_QW // LANES
KV_SLABS = B_GROUP * B_HDIM // LANES

_SPLITS = (A_KW, A_KW, A_VW, A_VW, B_QW, B_KVW, B_KVW, B_QW, D_MODEL, D_MODEL)
_OFFS = tuple(int(sum(_SPLITS[:i])) for i in range(len(_SPLITS)))
IN_COLS = int(sum(_SPLITS))

VMEM_LIMIT = 56 * 1024 * 1024

TM_IN = 256
TM_OUT = 512
PROJ_COLS = 256
HGRN_CHUNK = 64
HGRN_SUB = 16


def _sigmoid(x):
    return 1.0 / (1.0 + jnp.exp(-x))


def _silu(x):
    return x * _sigmoid(x)


def _dot_nt(a, b):
    return lax.dot_general(a, b, (((1,), (1,)), ((), ())), preferred_element_type=F32)


def _dot_tn(a, b, precision=None):
    return lax.dot_general(a, b, (((0,), (0,)), ((), ())), precision=precision, preferred_element_type=F32)


def _dot(a, b):
    return jnp.dot(a, b, preferred_element_type=F32)


def _log2(n):
    assert n & (n - 1) == 0
    return n.bit_length() - 1


def _inproj_kernel(x_ref, g_ref, lbl_ref, gn_ref, w_ref,
                   q_ref, k_ref, lf_ref, v_ref, za_ref, qb_ref, kv_ref, zb_ref, gg_ref):
    x = x_ref[...]
    ms = jnp.mean(x * x, axis=-1, keepdims=True)
    xn = (x * lax.rsqrt(ms + EPS) * g_ref[...]).astype(BF16)

    def seg(i, width=None):
        width = _SPLITS[i] if width is None else width
        return _dot(xn, w_ref[:, _OFFS[i]:_OFFS[i] + width])

    l = lbl_ref[...]
    e = jnp.exp(l - jnp.max(l, axis=0, keepdims=True))
    lb = e[0:1, :] / jnp.sum(e, axis=0, keepdims=True)

    q_ref[...] = _silu(seg(0)).astype(BF16)
    sig = _sigmoid(seg(1))
    k_ref[...] = ((1.0 - lb) * (1.0 - sig)).astype(BF16)
    lf_ref[...] = jnp.log2(lb + (1.0 - lb) * sig)
    v_ref[...] = seg(2).astype(BF16)
    za_ref[...] = (_silu(seg(3)) * gn_ref[...]).astype(BF16)
    qb = (seg(4) * Q_SCALE).astype(BF16)
    for p in range(Q_SLABS):
        qb_ref[p] = qb[:, p * LANES:(p + 1) * LANES]
    kv_ref[...] = seg(5, 2 * B_KVW)
    zb_ref[...] = _silu(seg(7)).astype(BF16)
    gg_ref[...] = _sigmoid(seg(8, 2 * D_MODEL)).astype(BF16)


def _in_projection(x2d, norm_in, lb_logits, a_gnorm, w_in_bf16, tm):
    m = x2d.shape[0]
    assert m % tm == 0
    row = lambda w: pl.BlockSpec((tm, w), lambda i: (i, 0))
    const = lambda shape: pl.BlockSpec(shape, lambda i: (0, 0), pipeline_mode=pl.Buffered(1))
    out_shapes = (
        jax.ShapeDtypeStruct((m, A_KW), BF16),
        jax.ShapeDtypeStruct((m, A_KW), BF16),
        jax.ShapeDtypeStruct((m, A_KW), F32),
        jax.ShapeDtypeStruct((m, A_VW), BF16),
        jax.ShapeDtypeStruct((m, A_VW), BF16),
        jax.ShapeDtypeStruct((Q_SLABS, m, LANES), BF16),
        jax.ShapeDtypeStruct((m, 2 * B_KVW), F32),
        jax.ShapeDtypeStruct((m, B_QW), BF16),
        jax.ShapeDtypeStruct((m, 2 * D_MODEL), BF16),
    )
    out_specs = tuple(pl.BlockSpec((Q_SLABS, tm, LANES), lambda i: (0, i, 0)) if len(s.shape) == 3
                      else row(s.shape[1]) for s in out_shapes)
    return pl.pallas_call(
        _inproj_kernel,
        grid=(m // tm,),
        in_specs=[row(D_MODEL), const((1, D_MODEL)), const(lb_logits.shape), const((1, A_VW)),
                  const((D_MODEL, IN_COLS))],
        out_specs=out_specs,
        out_shape=out_shapes,
        compiler_params=pltpu.CompilerParams(dimension_semantics=("arbitrary",), vmem_limit_bytes=VMEM_LIMIT),
        name="in_projection",
    )(x2d, norm_in.reshape(1, D_MODEL), lb_logits, jnp.tile(a_gnorm, A_HEADS).reshape(1, A_VW), w_in_bf16)


def _rows(vecs, sub):
    return jnp.concatenate([jnp.broadcast_to(v, (sub, v.shape[1])) for v in vecs], axis=0)


def _tri_blocks(chunk, sub):
    r = lax.broadcasted_iota(jnp.int32, (chunk, chunk), 0)
    s = lax.broadcasted_iota(jnp.int32, (chunk, chunk), 1)
    return jnp.where((s <= r) & (((r ^ s) & ~(sub - 1)) == 0), 1.0, 0.0).astype(BF16)


def _block_cumsum_mxu(x, tri):
    hi = x.astype(BF16)
    lo = (x - hi.astype(F32)).astype(BF16)
    return _dot(tri, hi) + _dot(tri, lo)


def _hgrn_operands(q, k, ell, chunk, sub):
    _log2(sub)
    n = chunk // sub
    qf = q.astype(F32)
    kf = k.astype(F32)
    tau = [ell[(i + 1) * sub - 1:(i + 1) * sub, :] for i in range(n)]
    mu = [ell[i * sub + sub // 2 - 1:i * sub + sub // 2, :] for i in range(n)]
    beta = [jnp.zeros_like(tau[0])]
    for i in range(n):
        beta.append(beta[i] + tau[i])
    total = beta[n]

    def brow(x):
        return jnp.broadcast_to(x, (sub, x.shape[1])).astype(BF16)

    qt = (qf * jnp.exp2(ell)).astype(BF16)
    kt = (kf * jnp.exp2(_rows(tau, sub) - ell)).astype(BF16)
    mu_rows = _rows(mu, sub)
    qd = (qf * jnp.exp2(ell - mu_rows)).astype(BF16)
    kd = (kf * jnp.exp2(mu_rows - ell)).astype(BF16)
    blk = lambda x, i: x[i * sub:(i + 1) * sub]
    qhat = jnp.concatenate([blk(qt, i) * brow(jnp.exp2(beta[i])) for i in range(n)], axis=0)
    khat = jnp.concatenate([blk(kt, i) * brow(jnp.exp2(total - beta[i + 1])) for i in range(n)], axis=0)
    zero_blk = jnp.zeros((sub, A_KDIM), BF16)
    qts, kins = [], []
    for i in range(1, n):
        kin = [blk(kt, j) * brow(jnp.exp2(beta[i] - beta[j + 1])) for j in range(i)]
        kins.append(jnp.concatenate(kin + [zero_blk] * (n - i), axis=0))
        qts.append(blk(qt, i))
    return dict(qd=qd, kd=kd, qhat=qhat, khat=khat, qts=qts, kins=kins, total=total)


def _hgrn_scores(ops, st, masks, state_is_kv):
    causal, same = masks
    sb = st.astype(BF16)
    o_inter = _dot(ops["qhat"], sb) if state_is_kv else _dot_nt(ops["qhat"], sb)
    att = jnp.where(causal, _dot_nt(ops["qd"], ops["kd"]), 0.0)
    if ops["qts"]:
        sub, chunk = ops["qts"][0].shape[0], att.shape[0]
        off = [jnp.zeros((sub, chunk), F32)] + [_dot_nt(qt, kin) for qt, kin in zip(ops["qts"], ops["kins"])]
        att = jnp.where(same, att, jnp.concatenate(off, axis=0))
    return o_inter, att.astype(BF16)


def _hgrn_update(ops, scores, v, st, lf, state_is_kv):
    o_inter, att = scores
    o = o_inter + _dot(att, v)
    if state_is_kv:
        total_col = _dot_tn(lf, jnp.ones_like(lf), precision=lax.Precision.HIGHEST)
        st_new = st * jnp.exp2(total_col) + _dot_tn(ops["khat"], v)
    else:
        st_new = st * jnp.exp2(ops["total"]) + _dot_tn(v, ops["khat"])
    return o, st_new


def _chunk_masks(chunk, sub):
    r = lax.broadcasted_iota(jnp.int32, (chunk, chunk), 0)
    s = lax.broadcasted_iota(jnp.int32, (chunk, chunk), 1)
    return s <= r, ((r ^ s) & ~(sub - 1)) == 0


def _hgrn_finish(o, z):
    return (o * lax.rsqrt(jnp.mean(o * o, axis=-1, keepdims=True) + EPS) * z.astype(F32)).astype(BF16)


def _hgrn_sample_kernel(q_ref, k_ref, v_ref, lf_ref, z_ref, s0_ref, o_ref, sout_ref, *, t_valid, seqs):
    tp = q_ref.shape[1]
    tri = _tri_blocks(tp, tp)
    masks = _chunk_masks(tp, tp)
    rows = lax.broadcasted_iota(jnp.int32, (tp, A_KW), 0)
    units = [(b, h, slice(h * A_KDIM, (h + 1) * A_KDIM)) for b in range(seqs) for h in range(A_HEADS)]
    lfs = [jnp.where(rows < t_valid, lf_ref[b], 0.0) for b in range(seqs)]
    ells = [_block_cumsum_mxu(lf, tri) for lf in lfs]
    ops = [_hgrn_operands(q_ref[b, :, cs], k_ref[b, :, cs], ells[b][:, cs], tp, tp) for b, h, cs in units]
    scores = [_hgrn_scores(op, s0_ref[b, h], masks, True) for op, (b, h, cs) in zip(ops, units)]
    outs = []
    for op, sc, (b, h, cs) in zip(ops, scores, units):
        o, sout_ref[b, h] = _hgrn_update(op, sc, v_ref[b, :, cs], s0_ref[b, h], lfs[b][:, cs], True)
        outs.append(o)
    for o, (b, h, cs) in zip(outs, units):
        o_ref[b, :, cs] = _hgrn_finish(o, z_ref[b, :, cs])


def _hgrn_sample(q, k, v, lf, z, state, *, t_valid, seqs):
    b, tp, _ = q.shape
    assert b % seqs == 0
    blk = lambda w: pl.BlockSpec((seqs, tp, w), lambda i: (i, 0, 0))
    sblk = pl.BlockSpec((seqs, A_HEADS, A_KDIM, A_VDIM), lambda i: (i, 0, 0, 0))
    return pl.pallas_call(
        functools.partial(_hgrn_sample_kernel, t_valid=t_valid, seqs=seqs),
        grid=(b // seqs,),
        in_specs=[blk(A_KW), blk(A_KW), blk(A_VW), blk(A_KW), blk(A_VW), sblk],
        out_specs=(blk(A_VW), sblk),
        out_shape=(jax.ShapeDtypeStruct((b, tp, A_VW), BF16),
                   jax.ShapeDtypeStruct((b, A_HEADS, A_KDIM, A_VDIM), F32)),
        compiler_params=pltpu.CompilerParams(dimension_semantics=("arbitrary",), vmem_limit_bytes=VMEM_LIMIT),
        name="hgrn2_sample",
    )(q, k, v, lf, z, state)


LOG2E = math.log2(math.e)
Q_SCALE = LOG2E / math.sqrt(B_HDIM)


def _alibi_slope(h):
    return 2.0 ** (-8.0 * (h + 1) / B_QHEADS)


def _lane_halves(x):
    low = lax.broadcasted_iota(jnp.int32, x.shape, 1) < B_HDIM
    swapped = pltpu.roll(x, B_HDIM, axis=1)
    both = (jnp.where(low, x, swapped), jnp.where(low, swapped, x))
    return [[jnp.where(low, h, 0.0).astype(BF16), jnp.where(low, 0.0, h).astype(BF16)] for h in both]


def _per_slab(slab, values):
    out = jnp.full(slab.shape, values[0], F32)
    for p in range(1, len(values)):
        out = jnp.where(slab == p, values[p], out)
    return out


def _sink_slot(tile):
    head = tile[:16]
    row0 = lax.broadcasted_iota(jnp.int32, head.shape, 0) == 0
    return jnp.concatenate([jnp.where(row0, jnp.zeros_like(head), head), tile[16:]], axis=0)


def _bias_table(kvh, par, sinks_ref, rows_per_slab, dist, valid, sink_in_col0):
    shift = _log2(rows_per_slab)
    heads = [kvh * B_GROUP + 2 * p + par for p in range(KV_SLABS)]
    slab = lax.broadcasted_iota(jnp.int32, dist.shape, 0) >> shift
    bias = jnp.where(valid, -_per_slab(slab, [_alibi_slope(h) * LOG2E for h in heads]) * dist.astype(F32), NEG)
    if sink_in_col0:
        col = lax.broadcasted_iota(jnp.int32, dist.shape, 1)
        bias = jnp.where(col == 0, _per_slab(slab, [sinks_ref[h] * LOG2E for h in heads]), bias)
    return bias


ROW_TILE = 128


def _attn_scores(q4, keys):
    return [[_dot_nt(q4, ks[par]) for ks in keys] for par in range(2)]


def _attn_probs(scores, bias_fns):
    rows = scores[0][0].shape[0]
    tile = min(ROW_TILE, rows)
    out = []
    for par in range(2):
        probs = [[] for _ in bias_fns]
        for r in range(0, rows, tile):
            rs = slice(r, r + tile)
            st = [s[rs] + bf(par, rs) for s, bf in zip(scores[par], bias_fns)]
            m = None
            for s in st:
                full = [s[:, c:c + LANES] for c in range(0, s.shape[1] - LANES + 1, LANES)]
                mx = jnp.max(functools.reduce(jnp.maximum, full) if full else s, axis=-1, keepdims=True)
                m = mx if m is None else jnp.maximum(m, mx)
            for i, s in enumerate(st):
                probs[i].append(jnp.exp2(s - m).astype(BF16))
        out.append([jnp.concatenate(p, axis=0) for p in probs])
    return out


def _attn_out(probs, values):
    o = None
    for par in range(2):
        acc = None
        for p, vs in zip(probs[par], values):
            part = _dot(p, jnp.concatenate([vs[par], jnp.ones_like(vs[par])], axis=1))
            acc = part if acc is None else acc + part
        part = acc[:, :LANES] / acc[:, LANES:]
        o = part if o is None else o + part
    return o


def _swa_prompt_kernel(sinks_ref, q_ref, kvp_ref, kvc_ref, z_ref, o_ref, bias_ref, *, nblk):
    w = WINDOW
    j = pl.program_id(1)

    @pl.when((pl.program_id(0) == 0) & (j == 0))
    def _():
        shape = (KV_SLABS * w, 2 * w)
        qi = lax.broadcasted_iota(jnp.int32, shape, 0) & (w - 1)
        kj = lax.broadcasted_iota(jnp.int32, shape, 1) - w
        dist = qi - kj
        band = (dist >= 0) & (dist < w)
        for kvh in range(B_KVHEADS):
            for par in range(2):
                bias_ref[0, kvh, par] = _bias_table(kvh, par, sinks_ref, w, dist, band & (kj >= 0), True)
                bias_ref[1, kvh, par] = _bias_table(kvh, par, sinks_ref, w, dist, band, True)

    def halves(kv):
        return _lane_halves(kv[:, :B_KVW]), _lane_halves(kv[:, B_KVW:])

    prev = halves(kvp_ref[0])
    for i in range(nblk):
        rs = slice(i * w, (i + 1) * w)
        cur = halves(kvc_ref[0, rs, :])
        sel = jnp.minimum(j, 1) if i == 0 else 1
        kvs = [[[jnp.concatenate([_sink_slot(prev[x][kvh][par]), cur[x][kvh][par]], axis=0)
                 for par in range(2)] for x in range(2)] for kvh in range(B_KVHEADS)]
        scores = [_attn_scores(q_ref[kvh * KV_SLABS:(kvh + 1) * KV_SLABS, rs, :].reshape(KV_SLABS * w, LANES),
                               [kvs[kvh][0]]) for kvh in range(B_KVHEADS)]
        probs = [_attn_probs(scores[kvh], [lambda par, rows, kvh=kvh, sel=sel: bias_ref[sel, kvh, par, rows, :]])
                 for kvh in range(B_KVHEADS)]
        for kvh in range(B_KVHEADS):
            o = _attn_out(probs[kvh], [kvs[kvh][1]])
            for p in range(KV_SLABS):
                cs = slice((kvh * KV_SLABS + p) * LANES, (kvh * KV_SLABS + p + 1) * LANES)
                o_ref[0, rs, cs] = (o[p * w:(p + 1) * w] * z_ref[0, rs, cs].astype(F32)).astype(BF16)
        prev = cur


def _swa_prompt(qb, kv, zb, sinks, *, nblk):
    b, t, _ = zb.shape
    tq = nblk * WINDOW
    assert t % tq == 0
    nq = t // tq
    return pl.pallas_call(
        functools.partial(_swa_prompt_kernel, nblk=nblk),
        grid=(b, nq),
        in_specs=[pl.BlockSpec(memory_space=pltpu.SMEM),
                  pl.BlockSpec((Q_SLABS, tq, LANES), lambda i, j: (0, i * nq + j, 0)),
                  pl.BlockSpec((1, WINDOW, 2 * B_KVW), lambda i, j: (i, jnp.maximum(j * nblk - 1, 0), 0)),
                  pl.BlockSpec((1, tq, 2 * B_KVW), lambda i, j: (i, j, 0)),
                  pl.BlockSpec((1, tq, B_QW), lambda i, j: (i, j, 0))],
        out_specs=pl.BlockSpec((1, tq, B_QW), lambda i, j: (i, j, 0)),
        out_shape=jax.ShapeDtypeStruct((b, t, B_QW), BF16),
        scratch_shapes=[pltpu.VMEM((2, B_KVHEADS, 2, KV_SLABS * WINDOW, 2 * WINDOW), F32)],
        compiler_params=pltpu.CompilerParams(dimension_semantics=("arbitrary", "arbitrary"),
                                             vmem_limit_bytes=VMEM_LIMIT),
        name="swa_prompt",
    )(sinks, qb, kv, kv, zb)


def _swa_sample_kernel(sinks_ref, q_ref, kvn_ref, ck_ref, cv_ref, z_ref, o_ref, cko_ref, cvo_ref,
                       biasc_ref, biasn_ref, *, t_new, seqs):
    w = ck_ref.shape[1]
    tp = kvn_ref.shape[1]
    assert 0 < t_new < WINDOW <= w

    @pl.when(pl.program_id(0) == 0)
    def _():
        t_c = lax.broadcasted_iota(jnp.int32, (KV_SLABS * tp, w), 0) & (tp - 1)
        d_c = t_c - (lax.broadcasted_iota(jnp.int32, (KV_SLABS * tp, w), 1) - w)
        t_n = lax.broadcasted_iota(jnp.int32, (KV_SLABS * tp, tp), 0) & (tp - 1)
        s_n = lax.broadcasted_iota(jnp.int32, (KV_SLABS * tp, tp), 1)
        d_n = t_n - s_n
        for kvh in range(B_KVHEADS):
            for par in range(2):
                biasc_ref[kvh, par] = _bias_table(kvh, par, sinks_ref, tp, d_c, (d_c >= 0) & (d_c < WINDOW), True)
                biasn_ref[kvh, par] = _bias_table(kvh, par, sinks_ref, tp, d_n,
                                                  (d_n >= 0) & (d_n < WINDOW) & (s_n < t_new), False)

    keys, values = [], []
    for b in range(seqs):
        ck, cv = ck_ref[b], cv_ref[b]
        kn, vn = kvn_ref[b, :, :B_KVW], kvn_ref[b, :, B_KVW:]
        cko_ref[b, 0:w - t_new, :] = ck[t_new:w, :]
        cko_ref[b, w - t_new:w, :] = kn[0:t_new, :]
        cvo_ref[b, 0:w - t_new, :] = cv[t_new:w, :]
        cvo_ref[b, w - t_new:w, :] = vn[0:t_new, :]
        kch, vch = [[[_sink_slot(t) for t in pair] for pair in _lane_halves(x)] for x in (ck, cv)]
        knh, vnh = _lane_halves(kn), _lane_halves(vn)
        keys.append([[kch[kvh], knh[kvh]] for kvh in range(B_KVHEADS)])
        values.append([[vch[kvh], vnh[kvh]] for kvh in range(B_KVHEADS)])
    units = [(b, kvh) for b in range(seqs) for kvh in range(B_KVHEADS)]
    scores = [_attn_scores(q_ref[kvh * KV_SLABS:(kvh + 1) * KV_SLABS, b * tp:(b + 1) * tp, :]
                           .reshape(KV_SLABS * tp, LANES), keys[b][kvh]) for b, kvh in units]
    probs = [_attn_probs(s, [lambda par, rows, kvh=kvh: biasc_ref[kvh, par, rows, :],
                             lambda par, rows, kvh=kvh: biasn_ref[kvh, par, rows, :]])
             for s, (b, kvh) in zip(scores, units)]
    outs = [_attn_out(p, values[b][kvh]) for p, (b, kvh) in zip(probs, units)]
    for o, (b, kvh) in zip(outs, units):
        for p in range(KV_SLABS):
            cs = slice((kvh * KV_SLABS + p) * LANES, (kvh * KV_SLABS + p + 1) * LANES)
            o_ref[b, :, cs] = (o[p * tp:(p + 1) * tp] * z_ref[b, :, cs].astype(F32)).astype(BF16)


def _swa_sample(qb, kvn, zb, sinks, cache_k, cache_v, *, t_new, seqs):
    b, tp, _ = zb.shape
    w = cache_k.shape[1]
    assert b % seqs == 0
    blk = lambda r, c: pl.BlockSpec((seqs, r, c), lambda i: (i, 0, 0))
    tbl = lambda s: pltpu.VMEM((B_KVHEADS, 2, KV_SLABS * tp, s), F32)
    return pl.pallas_call(
        functools.partial(_swa_sample_kernel, t_new=t_new, seqs=seqs),
        grid=(b // seqs,),
        in_specs=[pl.BlockSpec(memory_space=pltpu.SMEM),
                  pl.BlockSpec((Q_SLABS, seqs * tp, LANES), lambda i: (0, i, 0)),
                  blk(tp, 2 * B_KVW), blk(w, B_KVW), blk(w, B_KVW), blk(tp, B_QW)],
        out_specs=(blk(tp, B_QW), blk(w, B_KVW), blk(w, B_KVW)),
        out_shape=(jax.ShapeDtypeStruct((b, tp, B_QW), BF16),
                   jax.ShapeDtypeStruct((b, w, B_KVW), F32),
                   jax.ShapeDtypeStruct((b, w, B_KVW), F32)),
        scratch_shapes=[tbl(w), tbl(tp)],
        compiler_params=pltpu.CompilerParams(dimension_semantics=("arbitrary",), vmem_limit_bytes=VMEM_LIMIT),
        name="swa_sample",
    )(sinks, qb, kvn, cache_k, cache_v, zb)


def _merge_project(x, ha, hb, gg, wpa_ref, wpb_ref, wo_ref, gf):
    pa = _dot(ha, wpa_ref[...])
    pb = _dot(hb, wpb_ref[...])
    merged = gg[:, :D_MODEL].astype(F32) * pa + gg[:, D_MODEL:].astype(F32) * pb
    y = x + _dot(merged.astype(BF16), wo_ref[...])
    return y * lax.rsqrt(jnp.mean(y * y, axis=-1, keepdims=True) + EPS) * gf


def _outproj_kernel(x_ref, ha_ref, hb_ref, gg_ref, wpa_ref, wpb_ref, wo_ref, gf_ref, y_ref):
    y_ref[...] = _merge_project(x_ref[...], ha_ref[...], hb_ref[...], gg_ref[...], wpa_ref, wpb_ref, wo_ref,
                                gf_ref[...])


def _out_projection(x2d, ha, hb, gg, w_pa, w_pb, w_out, norm_final, tm):
    m = x2d.shape[0]
    assert m % tm == 0
    row = lambda w: pl.BlockSpec((tm, w), lambda i: (i, 0))
    const = lambda shape: pl.BlockSpec(shape, lambda i: (0, 0), pipeline_mode=pl.Buffered(1))
    return pl.pallas_call(
        _outproj_kernel,
        grid=(m // tm,),
        in_specs=[row(D_MODEL), row(A_VW), row(B_QW), row(2 * D_MODEL),
                  const((A_VW, D_MODEL)), const((B_QW, D_MODEL)), const((D_MODEL, D_MODEL)), const((1, D_MODEL))],
        out_specs=row(D_MODEL),
        out_shape=jax.ShapeDtypeStruct((m, D_MODEL), F32),
        compiler_params=pltpu.CompilerParams(dimension_semantics=("arbitrary",), vmem_limit_bytes=VMEM_LIMIT),
        name="out_projection",
    )(x2d, ha, hb, gg, w_pa, w_pb, w_out, norm_final.reshape(1, D_MODEL))


def _hgrn_out_kernel(q_ref, k_ref, v_ref, lf_ref, z_ref, hb_ref, gg_ref, x_ref, wpa_ref, wpb_ref, wo_ref, gf_ref,
                     y_ref, sout_ref, st_ref, ha_ref, *, chunk, sub, group):
    tb = q_ref.shape[1]
    nt = pl.program_id(1)

    @pl.when(nt == 0)
    def _():
        st_ref[...] = jnp.zeros_like(st_ref)

    tri = _tri_blocks(chunk, sub)
    masks = _chunk_masks(chunk, sub)
    cols = [slice(h * A_KDIM, (h + 1) * A_KDIM) for h in range(A_HEADS)]
    gf = gf_ref[...]

    def recur(r0):
        rs = slice(r0, r0 + chunk)
        carry = {}

        def first():
            ell = _block_cumsum_mxu(lf_ref[0, rs, :], tri)
            ops = [_hgrn_operands(q_ref[0, rs, cs], k_ref[0, rs, cs], ell[:, cs], chunk, sub) for cs in cols]
            carry["ops"] = ops
            carry["scores"] = [_hgrn_scores(ops[h], st_ref[h], masks, False) for h in range(A_HEADS)]

        def second():
            outs = []
            for h, cs in enumerate(cols):
                o, st_ref[h] = _hgrn_update(carry["ops"][h], carry["scores"][h], v_ref[0, rs, cs], st_ref[h],
                                            None, False)
                outs.append(o)
            for o, cs in zip(outs, cols):
                ha_ref[rs, cs] = _hgrn_finish(o, z_ref[0, rs, cs])

        return [first, second]

    def project(r0):
        rs = slice(r0, r0 + group)
        nb = D_MODEL // PROJ_COLS
        cb = [slice(n * PROJ_COLS, (n + 1) * PROJ_COLS) for n in range(nb)]
        merged, ys = [], []

        def branch(n):
            pa = _dot(ha_ref[rs, :], wpa_ref[:, cb[n]])
            pb = _dot(hb_ref[0, rs, :], wpb_ref[:, cb[n]])
            ga = gg_ref[0, rs, cb[n]].astype(F32)
            gb = gg_ref[0, rs, D_MODEL + n * PROJ_COLS:D_MODEL + (n + 1) * PROJ_COLS].astype(F32)
            merged.append((ga * pa + gb * pb).astype(BF16))

        def out(n):
            ys.append(x_ref[0, rs, cb[n]] + _dot(jnp.concatenate(merged, axis=1), wo_ref[:, cb[n]]))
            if n == nb - 1:
                ms = sum(jnp.sum(y * y, axis=-1, keepdims=True) for y in ys) * (1.0 / D_MODEL)
                scale = lax.rsqrt(ms + EPS)
                for m, y in enumerate(ys):
                    y_ref[0, rs, cb[m]] = y * scale * gf[:, cb[m]]

        return [functools.partial(branch, n) for n in range(nb)] + [functools.partial(out, n) for n in range(nb)]

    def interleave(a, b):
        n = max(len(a), len(b))
        for i in range(n):
            for steps in (a, b):
                lo, hi = i * len(steps) // n, (i + 1) * len(steps) // n
                for step in steps[lo:hi]:
                    step()

    pending = []
    for g in range(tb // group):
        steps = [s for c in range(group // chunk) for s in recur(g * group + c * chunk)]
        interleave(pending, steps)
        pending = project(g * group)
    interleave(pending, [])

    @pl.when(nt == pl.num_programs(1) - 1)
    def _():
        for h in range(A_HEADS):
            sout_ref[0, h] = st_ref[h].T


def _hgrn_out_prompt(q, k, v, lf, z, hb, gg, x, w_pa, w_pb, w_out, norm_final, *, tb, chunk, sub, group):
    b, t, _ = q.shape
    assert t % tb == 0 and tb % group == 0 and group % chunk == 0 and chunk % sub == 0
    blk = lambda w: pl.BlockSpec((1, tb, w), lambda i, j: (i, j, 0))
    const = lambda shape: pl.BlockSpec(shape, lambda i, j: (0, 0), pipeline_mode=pl.Buffered(1))
    return pl.pallas_call(
        functools.partial(_hgrn_out_kernel, chunk=chunk, sub=sub, group=group),
        grid=(b, t // tb),
        in_specs=[blk(A_KW), blk(A_KW), blk(A_VW), blk(A_KW), blk(A_VW), blk(B_QW), blk(2 * D_MODEL), blk(D_MODEL),
                  const((A_VW, D_MODEL)), const((B_QW, D_MODEL)), const((D_MODEL, D_MODEL)), const((1, D_MODEL))],
        out_specs=(blk(D_MODEL), pl.BlockSpec((1, A_HEADS, A_KDIM, A_VDIM), lambda i, j: (i, 0, 0, 0))),
        out_shape=(jax.ShapeDtypeStruct((b, t, D_MODEL), F32),
                   jax.ShapeDtypeStruct((b, A_HEADS, A_KDIM, A_VDIM), F32)),
        scratch_shapes=[pltpu.VMEM((A_HEADS, A_VDIM, A_KDIM), F32), pltpu.VMEM((tb, A_VW), BF16)],
        compiler_params=pltpu.CompilerParams(dimension_semantics=("arbitrary", "arbitrary"),
                                             vmem_limit_bytes=VMEM_LIMIT),
        name="hgrn2_out_prompt",
    )(q, k, v, lf, z, hb, gg, x, w_pa, w_pb, w_out, norm_final.reshape(1, D_MODEL))


def _layer(x, weights, *, hgrn_state, cache, t_valid):
    norm_in, w_in, lb_logits, a_gnorm, b_sinks, w_pa, w_pb, w_out, norm_final = weights
    b, t, d = x.shape
    x2d = x.reshape(b * t, d)
    q, k, lf, v, za, qb, kv, zb, gg = _in_projection(x2d, norm_in, lb_logits, a_gnorm, w_in, min(TM_IN, b * t))
    r3 = lambda a: a.reshape(b, t, a.shape[-1])
    kv3 = r3(kv)
    if cache is None:
        hb = _swa_prompt(qb, kv3, r3(zb), b_sinks, nblk=min(4, t // WINDOW))
        last = min(WINDOW, t)
        ck = kv3[:, t - last:, :B_KVW]
        cv = kv3[:, t - last:, B_KVW:]
        y, s_new = _hgrn_out_prompt(r3(q), r3(k), r3(v), r3(lf), r3(za), hb, r3(gg), x, w_pa, w_pb, w_out, norm_final,
                                    tb=min(512, t), chunk=HGRN_CHUNK, sub=HGRN_SUB, group=min(256, t))
        return y, s_new, ck, cv
    ha, s_new = _hgrn_sample(r3(q), r3(k), r3(v), r3(lf), r3(za), hgrn_state, t_valid=t_valid, seqs=min(4, b))
    hb, ck, cv = _swa_sample(qb, kv3, r3(zb), b_sinks, cache[0], cache[1], t_new=t_valid, seqs=min(8, b))
    y = _out_projection(x2d, ha.reshape(b * t, A_VW), hb.reshape(b * t, B_QW), gg, w_pa, w_pb, w_out, norm_final,
                        min(TM_OUT, b * t))
    return y.reshape(b, t, d), s_new, ck, cv


def kernel(x_prompt, x_sample, state_hgrn, cache_k, cache_v, norm_in, w_in, lb_logits, a_gnorm, b_sinks,
           w_pa, w_pb, w_out, norm_final):
    assert state_hgrn.shape[0] == 1, "single decoder layer"
    weights = (norm_in[0], w_in[0].astype(BF16), lb_logits, a_gnorm[0], b_sinks[0],
               w_pa[0].astype(BF16), w_pb[0].astype(BF16), w_out[0].astype(BF16), norm_final)
    y_p, s_p, ck_p, cv_p = _layer(x_prompt, weights, hgrn_state=None, cache=None, t_valid=None)
    bs, ts, _ = x_sample.shape
    w = cache_k.shape[2]
    xs = jnp.pad(x_sample, ((0, 0), (0, SAMPLE_PAD - ts), (0, 0)))
    cache = (cache_k[0].reshape(bs, w, B_KVW), cache_v[0].reshape(bs, w, B_KVW))
    y_s, s_s, ck_s, cv_s = _layer(xs, weights, hgrn_state=state_hgrn[0], cache=cache, t_valid=ts)
    kvshape = lambda a: a.reshape(1, a.shape[0], a.shape[1], B_KVHEADS, B_HDIM)
    return (y_p, y_s[:, :ts], s_p[None], kvshape(ck_p), kvshape(cv_p),
            s_s[None], kvshape(ck_s), kvshape(cv_s))
```

```python
import functools
import math

import jax
import jax.numpy as jnp
from jax import lax
from jax.experimental import pallas as pl
from jax.experimental.pallas import tpu as pltpu

F32 = jnp.float32
BF16 = jnp.bfloat16

D_MODEL = 1024
A_HEADS = 8
A_KDIM = 128
A_VDIM = D_MODEL // A_HEADS
A_KW = A_HEADS * A_KDIM
A_VW = A_HEADS * A_VDIM
B_QHEADS = 16
B_KVHEADS = 2
B_HDIM = 64
B_GROUP = B_QHEADS // B_KVHEADS
B_QW = B_QHEADS * B_HDIM
B_KVW = B_KVHEADS * B_HDIM
WINDOW = 128
EPS = 1e-6
NEG = -1e30
LANES = 128
SAMPLE_PAD = 16
Q_SLABS = B_QW // LANES
KV_SLABS = B_GROUP * B_HDIM // LANES

_SPLITS = (A_KW, A_KW, A_VW, A_VW, B_QW, B_KVW, B_KVW, B_QW, D_MODEL, D_MODEL)
_OFFS = tuple(int(sum(_SPLITS[:i])) for i in range(len(_SPLITS)))
IN_COLS = int(sum(_SPLITS))

VMEM_LIMIT = 56 * 1024 * 1024

TM_IN = 256
TM_OUT = 512
PROJ_COLS = 256
HGRN_CHUNK = 64
HGRN_SUB = 16


def _sigmoid(x):
    return 1.0 / (1.0 + jnp.exp(-x))


def _silu(x):
    return x * _sigmoid(x)


def _dot_nt(a, b):
    return lax.dot_general(a, b, (((1,), (1,)), ((), ())), preferred_element_type=F32)


def _dot_tn(a, b, precision=None):
    return lax.dot_general(a, b, (((0,), (0,)), ((), ())), precision=precision, preferred_element_type=F32)


def _dot(a, b):
    return jnp.dot(a, b, preferred_element_type=F32)


def _log2(n):
    assert n & (n - 1) == 0
    return n.bit_length() - 1


def _inproj_kernel(x_ref, g_ref, lbl_ref, gn_ref, w_ref,
                   q_ref, k_ref, lf_ref, v_ref, za_ref, qb_ref, kv_ref, zb_ref, gg_ref):
    x = x_ref[...]
    ms = jnp.mean(x * x, axis=-1, keepdims=True)
    xn = (x * lax.rsqrt(ms + EPS) * g_ref[...]).astype(BF16)

    def seg(i, width=None):
        width = _SPLITS[i] if width is None else width
        return _dot(xn, w_ref[:, _OFFS[i]:_OFFS[i] + width])

    l = lbl_ref[...]
    e = jnp.exp(l - jnp.max(l, axis=0, keepdims=True))
    lb = e[0:1, :] / jnp.sum(e, axis=0, keepdims=True)

    q_ref[...] = _silu(seg(0)).astype(BF16)
    sig = _sigmoid(seg(1))
    k_ref[...] = ((1.0 - lb) * (1.0 - sig)).astype(BF16)
    lf_ref[...] = jnp.log2(lb + (1.0 - lb) * sig)
    v_ref[...] = seg(2).astype(BF16)
    za_ref[...] = (_silu(seg(3)) * gn_ref[...]).astype(BF16)
    qb = (seg(4) * Q_SCALE).astype(BF16)
    for p in range(Q_SLABS):
        qb_ref[p] = qb[:, p * LANES:(p + 1) * LANES]
    kv_ref[...] = seg(5, 2 * B_KVW)
    zb_ref[...] = _silu(seg(7)).astype(BF16)
    gg_ref[...] = _sigmoid(seg(8, 2 * D_MODEL)).astype(BF16)


def _in_projection(x2d, norm_in, lb_logits, a_gnorm, w_in_bf16, tm):
    m = x2d.shape[0]
    assert m % tm == 0
    row = lambda w: pl.BlockSpec((tm, w), lambda i: (i, 0))
    const = lambda shape: pl.BlockSpec(shape, lambda i: (0, 0), pipeline_mode=pl.Buffered(1))
    out_shapes = (
        jax.ShapeDtypeStruct((m, A_KW), BF16),
        jax.ShapeDtypeStruct((m, A_KW), BF16),
        jax.ShapeDtypeStruct((m, A_KW), F32),
        jax.ShapeDtypeStruct((m, A_VW), BF16),
        jax.ShapeDtypeStruct((m, A_VW), BF16),
        jax.ShapeDtypeStruct((Q_SLABS, m, LANES), BF16),
        jax.ShapeDtypeStruct((m, 2 * B_KVW), F32),
        jax.ShapeDtypeStruct((m, B_QW), BF16),
        jax.ShapeDtypeStruct((m, 2 * D_MODEL), BF16),
    )
    out_specs = tuple(pl.BlockSpec((Q_SLABS, tm, LANES), lambda i: (0, i, 0)) if len(s.shape) == 3
                      else row(s.shape[1]) for s in out_shapes)
    return pl.pallas_call(
        _inproj_kernel,
        grid=(m // tm,),
        in_specs=[row(D_MODEL), const((1, D_MODEL)), const(lb_logits.shape), const((1, A_VW)),
                  const((D_MODEL, IN_COLS))],
        out_specs=out_specs,
        out_shape=out_shapes,
        compiler_params=pltpu.CompilerParams(dimension_semantics=("arbitrary",), vmem_limit_bytes=VMEM_LIMIT),
        name="in_projection",
    )(x2d, norm_in.reshape(1, D_MODEL), lb_logits, jnp.tile(a_gnorm, A_HEADS).reshape(1, A_VW), w_in_bf16)


def _rows(vecs, sub):
    return jnp.concatenate([jnp.broadcast_to(v, (sub, v.shape[1])) for v in vecs], axis=0)


def _tri_blocks(chunk, sub):
    r = lax.broadcasted_iota(jnp.int32, (chunk, chunk), 0)
    s = lax.broadcasted_iota(jnp.int32, (chunk, chunk), 1)
    return jnp.where((s <= r) & (((r ^ s) & ~(sub - 1)) == 0), 1.0, 0.0).astype(BF16)


def _block_cumsum_mxu(x, tri):
    hi = x.astype(BF16)
    lo = (x - hi.astype(F32)).astype(BF16)
    return _dot(tri, hi) + _dot(tri, lo)


def _hgrn_operands(q, k, ell, chunk, sub):
    _log2(sub)
    n = chunk // sub
    qf = q.astype(F32)
    kf = k.astype(F32)
    tau = [ell[(i + 1) * sub - 1:(i + 1) * sub, :] for i in range(n)]
    mu = [ell[i * sub + sub // 2 - 1:i * sub + sub // 2, :] for i in range(n)]
    beta = [jnp.zeros_like(tau[0])]
    for i in range(n):
        beta.append(beta[i] + tau[i])
    total = beta[n]

    def brow(x):
        return jnp.broadcast_to(x, (sub, x.shape[1])).astype(BF16)

    qt = (qf * jnp.exp2(ell)).astype(BF16)
    kt = (kf * jnp.exp2(_rows(tau, sub) - ell)).astype(BF16)
    mu_rows = _rows(mu, sub)
    qd = (qf * jnp.exp2(ell - mu_rows)).astype(BF16)
    kd = (kf * jnp.exp2(mu_rows - ell)).astype(BF16)
    blk = lambda x, i: x[i * sub:(i + 1) * sub]
    qhat = jnp.concatenate([blk(qt, i) * brow(jnp.exp2(beta[i])) for i in range(n)], axis=0)
    khat = jnp.concatenate([blk(kt, i) * brow(jnp.exp2(total - beta[i + 1])) for i in range(n)], axis=0)
    zero_blk = jnp.zeros((sub, A_KDIM), BF16)
    qts, kins = [], []
    for i in range(1, n):
        kin = [blk(kt, j) * brow(jnp.exp2(beta[i] - beta[j + 1])) for j in range(i)]
        kins.append(jnp.concatenate(kin + [zero_blk] * (n - i), axis=0))
        qts.append(blk(qt, i))
    return dict(qd=qd, kd=kd, qhat=qhat, khat=khat, qts=qts, kins=kins, total=total)


def _hgrn_scores(ops, st, masks, state_is_kv):
    causal, same = masks
    sb = st.astype(BF16)
    o_inter = _dot(ops["qhat"], sb) if state_is_kv else _dot_nt(ops["qhat"], sb)
    att = jnp.where(causal, _dot_nt(ops["qd"], ops["kd"]), 0.0)
    if ops["qts"]:
        sub, chunk = ops["qts"][0].shape[0], att.shape[0]
        off = [jnp.zeros((sub, chunk), F32)] + [_dot_nt(qt, kin) for qt, kin in zip(ops["qts"], ops["kins"])]
        att = jnp.where(same, att, jnp.concatenate(off, axis=0))
    return o_inter, att.astype(BF16)


def _column_broadcast(row):
    rows = lax.broadcasted_iota(jnp.int32, (16, row.shape[1]), 0)
    parts = jnp.zeros(rows.shape, F32)
    rest = row
    for i in range(3):
        part = rest.astype(BF16).astype(F32)
        parts = jnp.where(rows == i, part, parts)
        rest = rest - part
    return _dot_tn(parts.astype(BF16), jnp.where(rows < 3, 1.0, 0.0).astype(BF16))


def _hgrn_update(ops, scores, v, st, state_is_kv):
    o_inter, att = scores
    o = o_inter + _dot(att, v)
    decay = jnp.exp2(ops["total"])
    if state_is_kv:
        st_new = st * _column_broadcast(decay) + _dot_tn(ops["khat"], v)
    else:
        st_new = st * decay + _dot_tn(v, ops["khat"])
    return o, st_new


def _chunk_masks(chunk, sub):
    r = lax.broadcasted_iota(jnp.int32, (chunk, chunk), 0)
    s = lax.broadcasted_iota(jnp.int32, (chunk, chunk), 1)
    return s <= r, ((r ^ s) & ~(sub - 1)) == 0


def _hgrn_finish(o, z):
    return (o * lax.rsqrt(jnp.mean(o * o, axis=-1, keepdims=True) + EPS) * z.astype(F32)).astype(BF16)


def _hgrn_sample_kernel(q_ref, k_ref, v_ref, lf_ref, z_ref, s0_ref, o_ref, sout_ref, *, t_valid, seqs):
    tp = q_ref.shape[1]
    tri = _tri_blocks(tp, tp)
    masks = _chunk_masks(tp, tp)
    rows = lax.broadcasted_iota(jnp.int32, (tp, A_KW), 0)
    units = [(b, h, slice(h * A_KDIM, (h + 1) * A_KDIM)) for b in range(seqs) for h in range(A_HEADS)]
    lfs = [jnp.where(rows < t_valid, lf_ref[b], 0.0) for b in range(seqs)]
    ells = [_block_cumsum_mxu(lf, tri) for lf in lfs]
    ops = [_hgrn_operands(q_ref[b, :, cs], k_ref[b, :, cs], ells[b][:, cs], tp, tp) for b, h, cs in units]
    scores = [_hgrn_scores(op, s0_ref[b, h], masks, True) for op, (b, h, cs) in zip(ops, units)]
    outs = []
    for op, sc, (b, h, cs) in zip(ops, scores, units):
        o, sout_ref[b, h] = _hgrn_update(op, sc, v_ref[b, :, cs], s0_ref[b, h], True)
        outs.append(o)
    for o, (b, h, cs) in zip(outs, units):
        o_ref[b, :, cs] = _hgrn_finish(o, z_ref[b, :, cs])


def _hgrn_sample(q, k, v, lf, z, state, *, t_valid, seqs):
    b, tp, _ = q.shape
    assert b % seqs == 0
    blk = lambda w: pl.BlockSpec((seqs, tp, w), lambda i: (i, 0, 0))
    sblk = pl.BlockSpec((seqs, A_HEADS, A_KDIM, A_VDIM), lambda i: (i, 0, 0, 0))
    return pl.pallas_call(
        functools.partial(_hgrn_sample_kernel, t_valid=t_valid, seqs=seqs),
        grid=(b // seqs,),
        in_specs=[blk(A_KW), blk(A_KW), blk(A_VW), blk(A_KW), blk(A_VW), sblk],
        out_specs=(blk(A_VW), sblk),
        out_shape=(jax.ShapeDtypeStruct((b, tp, A_VW), BF16),
                   jax.ShapeDtypeStruct((b, A_HEADS, A_KDIM, A_VDIM), F32)),
        compiler_params=pltpu.CompilerParams(dimension_semantics=("arbitrary",), vmem_limit_bytes=VMEM_LIMIT),
        name="hgrn2_sample",
    )(q, k, v, lf, z, state)


LOG2E = math.log2(math.e)
Q_SCALE = LOG2E / math.sqrt(B_HDIM)


def _alibi_slope(h):
    return 2.0 ** (-8.0 * (h + 1) / B_QHEADS)


def _lane_halves(x):
    low = lax.broadcasted_iota(jnp.int32, x.shape, 1) < B_HDIM
    swapped = pltpu.roll(x, B_HDIM, axis=1)
    both = (jnp.where(low, x, swapped), jnp.where(low, swapped, x))
    return [[jnp.where(low, h, 0.0).astype(BF16), jnp.where(low, 0.0, h).astype(BF16)] for h in both]


def _per_slab(slab, values):
    out = jnp.full(slab.shape, values[0], F32)
    for p in range(1, len(values)):
        out = jnp.where(slab == p, values[p], out)
    return out


def _sink_slot(tile):
    head = tile[:16]
    row0 = lax.broadcasted_iota(jnp.int32, head.shape, 0) == 0
    return jnp.concatenate([jnp.where(row0, jnp.zeros_like(head), head), tile[16:]], axis=0)


def _bias_table(kvh, par, sinks_ref, rows_per_slab, dist, valid, sink_in_col0):
    shift = _log2(rows_per_slab)
    heads = [kvh * B_GROUP + 2 * p + par for p in range(KV_SLABS)]
    slab = lax.broadcasted_iota(jnp.int32, dist.shape, 0) >> shift
    bias = jnp.where(valid, -_per_slab(slab, [_alibi_slope(h) * LOG2E for h in heads]) * dist.astype(F32), NEG)
    if sink_in_col0:
        col = lax.broadcasted_iota(jnp.int32, dist.shape, 1)
        bias = jnp.where(col == 0, _per_slab(slab, [sinks_ref[h] * LOG2E for h in heads]), bias)
    return bias


ROW_TILE = 128


def _attn_scores(q4, keys):
    return [[_dot_nt(q4, ks[par]) for ks in keys] for par in range(2)]


def _attn_probs(scores, bias_fns):
    rows = scores[0][0].shape[0]
    tile = min(ROW_TILE, rows)
    out = []
    for par in range(2):
        probs = [[] for _ in bias_fns]
        for r in range(0, rows, tile):
            rs = slice(r, r + tile)
            st = [s[rs] + bf(par, rs) for s, bf in zip(scores[par], bias_fns)]
            m = None
            for s in st:
                full = [s[:, c:c + LANES] for c in range(0, s.shape[1] - LANES + 1, LANES)]
                mx = jnp.max(functools.reduce(jnp.maximum, full) if full else s, axis=-1, keepdims=True)
                m = mx if m is None else jnp.maximum(m, mx)
            for i, s in enumerate(st):
                probs[i].append(jnp.exp2(s - m).astype(BF16))
        out.append([jnp.concatenate(p, axis=0) for p in probs])
    return out


def _attn_out(probs, values):
    o = None
    for par in range(2):
        acc = None
        for p, vs in zip(probs[par], values):
            part = _dot(p, jnp.concatenate([vs[par], jnp.ones_like(vs[par])], axis=1))
            acc = part if acc is None else acc + part
        part = acc[:, :LANES] / acc[:, LANES:]
        o = part if o is None else o + part
    return o


def _swa_prompt_kernel(sinks_ref, q_ref, kvp_ref, kvc_ref, z_ref, o_ref, bias_ref, *, nblk):
    w = WINDOW
    j = pl.program_id(1)

    @pl.when((pl.program_id(0) == 0) & (j == 0))
    def _():
        shape = (KV_SLABS * w, 2 * w)
        qi = lax.broadcasted_iota(jnp.int32, shape, 0) & (w - 1)
        kj = lax.broadcasted_iota(jnp.int32, shape, 1) - w
        dist = qi - kj
        band = (dist >= 0) & (dist < w)
        for kvh in range(B_KVHEADS):
            for par in range(2):
                bias_ref[0, kvh, par] = _bias_table(kvh, par, sinks_ref, w, dist, band & (kj >= 0), True)
                bias_ref[1, kvh, par] = _bias_table(kvh, par, sinks_ref, w, dist, band, True)

    def halves(kv):
        return _lane_halves(kv[:, :B_KVW]), _lane_halves(kv[:, B_KVW:])

    prev = halves(kvp_ref[0])
    for i in range(nblk):
        rs = slice(i * w, (i + 1) * w)
        cur = halves(kvc_ref[0, rs, :])
        sel = jnp.minimum(j, 1) if i == 0 else 1
        kvs = [[[jnp.concatenate([_sink_slot(prev[x][kvh][par]), cur[x][kvh][par]], axis=0)
                 for par in range(2)] for x in range(2)] for kvh in range(B_KVHEADS)]
        scores = [_attn_scores(q_ref[kvh * KV_SLABS:(kvh + 1) * KV_SLABS, rs, :].reshape(KV_SLABS * w, LANES),
                               [kvs[kvh][0]]) for kvh in range(B_KVHEADS)]
        probs = [_attn_probs(scores[kvh], [lambda par, rows, kvh=kvh, sel=sel: bias_ref[sel, kvh, par, rows, :]])
                 for kvh in range(B_KVHEADS)]
        for kvh in range(B_KVHEADS):
            o = _attn_out(probs[kvh], [kvs[kvh][1]])
            for p in range(KV_SLABS):
                cs = slice((kvh * KV_SLABS + p) * LANES, (kvh * KV_SLABS + p + 1) * LANES)
                o_ref[0, rs, cs] = (o[p * w:(p + 1) * w] * z_ref[0, rs, cs].astype(F32)).astype(BF16)
        prev = cur


def _swa_prompt(qb, kv, zb, sinks, *, nblk):
    b, t, _ = zb.shape
    tq = nblk * WINDOW
    assert t % tq == 0
    nq = t // tq
    return pl.pallas_call(
        functools.partial(_swa_prompt_kernel, nblk=nblk),
        grid=(b, nq),
        in_specs=[pl.BlockSpec(memory_space=pltpu.SMEM),
                  pl.BlockSpec((Q_SLABS, tq, LANES), lambda i, j: (0, i * nq + j, 0)),
                  pl.BlockSpec((1, WINDOW, 2 * B_KVW), lambda i, j: (i, jnp.maximum(j * nblk - 1, 0), 0)),
                  pl.BlockSpec((1, tq, 2 * B_KVW), lambda i, j: (i, j, 0)),
                  pl.BlockSpec((1, tq, B_QW), lambda i, j: (i, j, 0))],
        out_specs=pl.BlockSpec((1, tq, B_QW), lambda i, j: (i, j, 0)),
        out_shape=jax.ShapeDtypeStruct((b, t, B_QW), BF16),
        scratch_shapes=[pltpu.VMEM((2, B_KVHEADS, 2, KV_SLABS * WINDOW, 2 * WINDOW), F32)],
        compiler_params=pltpu.CompilerParams(dimension_semantics=("arbitrary", "arbitrary"),
                                             vmem_limit_bytes=VMEM_LIMIT),
        name="swa_prompt",
    )(sinks, qb, kv, kv, zb)


def _swa_sample_kernel(sinks_ref, q_ref, kvn_ref, ck_ref, cv_ref, z_ref, o_ref, cko_ref, cvo_ref,
                       biasc_ref, biasn_ref, *, t_new, seqs):
    w = ck_ref.shape[1]
    tp = kvn_ref.shape[1]
    assert 0 < t_new < WINDOW <= w

    @pl.when(pl.program_id(0) == 0)
    def _():
        t_c = lax.broadcasted_iota(jnp.int32, (KV_SLABS * tp, w), 0) & (tp - 1)
        d_c = t_c - (lax.broadcasted_iota(jnp.int32, (KV_SLABS * tp, w), 1) - w)
        t_n = lax.broadcasted_iota(jnp.int32, (KV_SLABS * tp, tp), 0) & (tp - 1)
        s_n = lax.broadcasted_iota(jnp.int32, (KV_SLABS * tp, tp), 1)
        d_n = t_n - s_n
        for kvh in range(B_KVHEADS):
            for par in range(2):
                biasc_ref[kvh, par] = _bias_table(kvh, par, sinks_ref, tp, d_c, (d_c >= 0) & (d_c < WINDOW), True)
                biasn_ref[kvh, par] = _bias_table(kvh, par, sinks_ref, tp, d_n,
                                                  (d_n >= 0) & (d_n < WINDOW) & (s_n < t_new), False)

    keys, values = [], []
    for b in range(seqs):
        ck, cv = ck_ref[b], cv_ref[b]
        kn, vn = kvn_ref[b, :, :B_KVW], kvn_ref[b, :, B_KVW:]
        cko_ref[b, 0:w - t_new, :] = ck[t_new:w, :]
        cko_ref[b, w - t_new:w, :] = kn[0:t_new, :]
        cvo_ref[b, 0:w - t_new, :] = cv[t_new:w, :]
        cvo_ref[b, w - t_new:w, :] = vn[0:t_new, :]
        kch, vch = [[[_sink_slot(t) for t in pair] for pair in _lane_halves(x)] for x in (ck, cv)]
        knh, vnh = _lane_halves(kn), _lane_halves(vn)
        keys.append([[kch[kvh], knh[kvh]] for kvh in range(B_KVHEADS)])
        values.append([[vch[kvh], vnh[kvh]] for kvh in range(B_KVHEADS)])
    units = [(b, kvh) for b in range(seqs) for kvh in range(B_KVHEADS)]
    scores = [_attn_scores(q_ref[kvh * KV_SLABS:(kvh + 1) * KV_SLABS, b * tp:(b + 1) * tp, :]
                           .reshape(KV_SLABS * tp, LANES), keys[b][kvh]) for b, kvh in units]
    probs = [_attn_probs(s, [lambda par, rows, kvh=kvh: biasc_ref[kvh, par, rows, :],
                             lambda par, rows, kvh=kvh: biasn_ref[kvh, par, rows, :]])
             for s, (b, kvh) in zip(scores, units)]
    outs = [_attn_out(p, values[b][kvh]) for p, (b, kvh) in zip(probs, units)]
    for o, (b, kvh) in zip(outs, units):
        for p in range(KV_SLABS):
            cs = slice((kvh * KV_SLABS + p) * LANES, (kvh * KV_SLABS + p + 1) * LANES)
            o_ref[b, :, cs] = (o[p * tp:(p + 1) * tp] * z_ref[b, :, cs].astype(F32)).astype(BF16)


def _swa_sample(qb, kvn, zb, sinks, cache_k, cache_v, *, t_new, seqs):
    b, tp, _ = zb.shape
    w = cache_k.shape[1]
    assert b % seqs == 0
    blk = lambda r, c: pl.BlockSpec((seqs, r, c), lambda i: (i, 0, 0))
    tbl = lambda s: pltpu.VMEM((B_KVHEADS, 2, KV_SLABS * tp, s), F32)
    return pl.pallas_call(
        functools.partial(_swa_sample_kernel, t_new=t_new, seqs=seqs),
        grid=(b // seqs,),
        in_specs=[pl.BlockSpec(memory_space=pltpu.SMEM),
                  pl.BlockSpec((Q_SLABS, seqs * tp, LANES), lambda i: (0, i, 0)),
                  blk(tp, 2 * B_KVW), blk(w, B_KVW), blk(w, B_KVW), blk(tp, B_QW)],
        out_specs=(blk(tp, B_QW), blk(w, B_KVW), blk(w, B_KVW)),
        out_shape=(jax.ShapeDtypeStruct((b, tp, B_QW), BF16),
                   jax.ShapeDtypeStruct((b, w, B_KVW), F32),
                   jax.ShapeDtypeStruct((b, w, B_KVW), F32)),
        scratch_shapes=[tbl(w), tbl(tp)],
        compiler_params=pltpu.CompilerParams(dimension_semantics=("arbitrary",), vmem_limit_bytes=VMEM_LIMIT),
        name="swa_sample",
    )(sinks, qb, kvn, cache_k, cache_v, zb)


def _merge_project(x, ha, hb, gg, wpa_ref, wpb_ref, wo_ref, gf):
    pa = _dot(ha, wpa_ref[...])
    pb = _dot(hb, wpb_ref[...])
    merged = gg[:, :D_MODEL].astype(F32) * pa + gg[:, D_MODEL:].astype(F32) * pb
    y = x + _dot(merged.astype(BF16), wo_ref[...])
    return y * lax.rsqrt(jnp.mean(y * y, axis=-1, keepdims=True) + EPS) * gf


def _outproj_kernel(x_ref, ha_ref, hb_ref, gg_ref, wpa_ref, wpb_ref, wo_ref, gf_ref, y_ref):
    y_ref[...] = _merge_project(x_ref[...], ha_ref[...], hb_ref[...], gg_ref[...], wpa_ref, wpb_ref, wo_ref,
                                gf_ref[...])


def _out_projection(x2d, ha, hb, gg, w_pa, w_pb, w_out, norm_final, tm):
    m = x2d.shape[0]
    assert m % tm == 0
    row = lambda w: pl.BlockSpec((tm, w), lambda i: (i, 0))
    const = lambda shape: pl.BlockSpec(shape, lambda i: (0, 0), pipeline_mode=pl.Buffered(1))
    return pl.pallas_call(
        _outproj_kernel,
        grid=(m // tm,),
        in_specs=[row(D_MODEL), row(A_VW), row(B_QW), row(2 * D_MODEL),
                  const((A_VW, D_MODEL)), const((B_QW, D_MODEL)), const((D_MODEL, D_MODEL)), const((1, D_MODEL))],
        out_specs=row(D_MODEL),
        out_shape=jax.ShapeDtypeStruct((m, D_MODEL), F32),
        compiler_params=pltpu.CompilerParams(dimension_semantics=("arbitrary",), vmem_limit_bytes=VMEM_LIMIT),
        name="out_projection",
    )(x2d, ha, hb, gg, w_pa, w_pb, w_out, norm_final.reshape(1, D_MODEL))


def _hgrn_out_kernel(q_ref, k_ref, v_ref, lf_ref, z_ref, hb_ref, gg_ref, x_ref, wpa_ref, wpb_ref, wo_ref, gf_ref,
                     y_ref, sout_ref, st_ref, ha_ref, *, chunk, sub, group):
    tb = q_ref.shape[1]
    nt = pl.program_id(1)

    @pl.when(nt == 0)
    def _():
        st_ref[...] = jnp.zeros_like(st_ref)

    tri = _tri_blocks(chunk, sub)
    masks = _chunk_masks(chunk, sub)
    cols = [slice(h * A_KDIM, (h + 1) * A_KDIM) for h in range(A_HEADS)]
    gf = gf_ref[...]

    def recur(r0):
        rs = slice(r0, r0 + chunk)
        carry = {}

        def first():
            ell = _block_cumsum_mxu(lf_ref[0, rs, :], tri)
            ops = [_hgrn_operands(q_ref[0, rs, cs], k_ref[0, rs, cs], ell[:, cs], chunk, sub) for cs in cols]
            carry["ops"] = ops
            carry["scores"] = [_hgrn_scores(ops[h], st_ref[h], masks, False) for h in range(A_HEADS)]

        def second():
            outs = []
            for h, cs in enumerate(cols):
                o, st_ref[h] = _hgrn_update(carry["ops"][h], carry["scores"][h], v_ref[0, rs, cs], st_ref[h], False)
                outs.append(o)
            for o, cs in zip(outs, cols):
                ha_ref[rs, cs] = _hgrn_finish(o, z_ref[0, rs, cs])

        return [first, second]

    def project(r0):
        rs = slice(r0, r0 + group)
        nb = D_MODEL // PROJ_COLS
        cb = [slice(n * PROJ_COLS, (n + 1) * PROJ_COLS) for n in range(nb)]
        merged, ys = [], []

        def branch(n):
            pa = _dot(ha_ref[rs, :], wpa_ref[:, cb[n]])
            pb = _dot(hb_ref[0, rs, :], wpb_ref[:, cb[n]])
            ga = gg_ref[0, rs, cb[n]].astype(F32)
            gb = gg_ref[0, rs, D_MODEL + n * PROJ_COLS:D_MODEL + (n + 1) * PROJ_COLS].astype(F32)
            merged.append((ga * pa + gb * pb).astype(BF16))

        def out(n):
            ys.append(x_ref[0, rs, cb[n]] + _dot(jnp.concatenate(merged, axis=1), wo_ref[:, cb[n]]))
            if n == nb - 1:
                ms = sum(jnp.sum(y * y, axis=-1, keepdims=True) for y in ys) * (1.0 / D_MODEL)
                scale = lax.rsqrt(ms + EPS)
                for m, y in enumerate(ys):
                    y_ref[0, rs, cb[m]] = y * scale * gf[:, cb[m]]

        return [functools.partial(branch, n) for n in range(nb)] + [functools.partial(out, n) for n in range(nb)]

    def interleave(a, b):
        n = max(len(a), len(b))
        for i in range(n):
            for steps in (a, b):
                lo, hi = i * len(steps) // n, (i + 1) * len(steps) // n
                for step in steps[lo:hi]:
                    step()

    pending = []
    for g in range(tb // group):
        steps = [s for c in range(group // chunk) for s in recur(g * group + c * chunk)]
        interleave(pending, steps)
        pending = project(g * group)
    interleave(pending, [])

    @pl.when(nt == pl.num_programs(1) - 1)
    def _():
        for h in range(A_HEADS):
            sout_ref[0, h] = st_ref[h].T


def _hgrn_out_prompt(q, k, v, lf, z, hb, gg, x, w_pa, w_pb, w_out, norm_final, *, tb, chunk, sub, group):
    b, t, _ = q.shape
    assert t % tb == 0 and tb % group == 0 and group % chunk == 0 and chunk % sub == 0
    blk = lambda w: pl.BlockSpec((1, tb, w), lambda i, j: (i, j, 0))
    const = lambda shape: pl.BlockSpec(shape, lambda i, j: (0, 0), pipeline_mode=pl.Buffered(1))
    return pl.pallas_call(
        functools.partial(_hgrn_out_kernel, chunk=chunk, sub=sub, group=group),
        grid=(b, t // tb),
        in_specs=[blk(A_KW), blk(A_KW), blk(A_VW), blk(A_KW), blk(A_VW), blk(B_QW), blk(2 * D_MODEL), blk(D_MODEL),
                  const((A_VW, D_MODEL)), const((B_QW, D_MODEL)), const((D_MODEL, D_MODEL)), const((1, D_MODEL))],
        out_specs=(blk(D_MODEL), pl.BlockSpec((1, A_HEADS, A_KDIM, A_VDIM), lambda i, j: (i, 0, 0, 0))),
        out_shape=(jax.ShapeDtypeStruct((b, t, D_MODEL), F32),
                   jax.ShapeDtypeStruct((b, A_HEADS, A_KDIM, A_VDIM), F32)),
        scratch_shapes=[pltpu.VMEM((A_HEADS, A_VDIM, A_KDIM), F32), pltpu.VMEM((tb, A_VW), BF16)],
        compiler_params=pltpu.CompilerParams(dimension_semantics=("arbitrary", "arbitrary"),
                                             vmem_limit_bytes=VMEM_LIMIT),
        name="hgrn2_out_prompt",
    )(q, k, v, lf, z, hb, gg, x, w_pa, w_pb, w_out, norm_final.reshape(1, D_MODEL))


def _layer(x, weights, *, hgrn_state, cache, t_valid):
    norm_in, w_in, lb_logits, a_gnorm, b_sinks, w_pa, w_pb, w_out, norm_final = weights
    b, t, d = x.shape
    x2d = x.reshape(b * t, d)
    q, k, lf, v, za, qb, kv, zb, gg = _in_projection(x2d, norm_in, lb_logits, a_gnorm, w_in, min(TM_IN, b * t))
    r3 = lambda a: a.reshape(b, t, a.shape[-1])
    kv3 = r3(kv)
    if cache is None:
        hb = _swa_prompt(qb, kv3, r3(zb), b_sinks, nblk=min(4, t // WINDOW))
        last = min(WINDOW, t)
        ck = kv3[:, t - last:, :B_KVW]
        cv = kv3[:, t - last:, B_KVW:]
        y, s_new = _hgrn_out_prompt(r3(q), r3(k), r3(v), r3(lf), r3(za), hb, r3(gg), x, w_pa, w_pb, w_out, norm_final,
                                    tb=min(512, t), chunk=HGRN_CHUNK, sub=HGRN_SUB, group=min(256, t))
        return y, s_new, ck, cv
    ha, s_new = _hgrn_sample(r3(q), r3(k), r3(v), r3(lf), r3(za), hgrn_state, t_valid=t_valid, seqs=min(8, b))
    hb, ck, cv = _swa_sample(qb, kv3, r3(zb), b_sinks, cache[0], cache[1], t_new=t_valid, seqs=min(8, b))
    y = _out_projection(x2d, ha.reshape(b * t, A_VW), hb.reshape(b * t, B_QW), gg, w_pa, w_pb, w_out, norm_final,
                        min(TM_OUT, b * t))
    return y.reshape(b, t, d), s_new, ck, cv


def kernel(x_prompt, x_sample, state_hgrn, cache_k, cache_v, norm_in, w_in, lb_logits, a_gnorm, b_sinks,
           w_pa, w_pb, w_out, norm_final):
    assert state_hgrn.shape[0] == 1, "single decoder layer"
    weights = (norm_in[0], w_in[0].astype(BF16), lb_logits, a_gnorm[0], b_sinks[0],
               w_pa[0].astype(BF16), w_pb[0].astype(BF16), w_out[0].astype(BF16), norm_final)
    y_p, s_p, ck_p, cv_p = _layer(x_prompt, weights, hgrn_state=None, cache=None, t_valid=None)
    bs, ts, _ = x_sample.shape
    w = cache_k.shape[2]
    xs = jnp.pad(x_sample, ((0, 0), (0, SAMPLE_PAD - ts), (0, 0)))
    cache = (cache_k[0].reshape(bs, w, B_KVW), cache_v[0].reshape(bs, w, B_KVW))
    y_s, s_s, ck_s, cv_s = _layer(xs, weights, hgrn_state=state_hgrn[0], cache=cache, t_valid=ts)
    kvshape = lambda a: a.reshape(1, a.shape[0], a.shape[1], B_KVHEADS, B_HDIM)
    return (y_p, y_s[:, :ts], s_p[None], kvshape(ck_p), kvshape(cv_p),
            s_s[None], kvshape(ck_s), kvshape(cv_s))
```

```python
import functools
import math

import jax
import jax.numpy as jnp
from jax import lax
from jax.experimental import pallas as pl
from jax.experimental.pallas import tpu as pltpu

F32 = jnp.float32
BF16 = jnp.bfloat16

D_MODEL = 1024
A_HEADS = 8
A_KDIM = 128
A_VDIM = D_MODEL // A_HEADS
A_KW = A_HEADS * A_KDIM
A_VW = A_HEADS * A_VDIM
B_QHEADS = 16
B_KVHEADS = 2
B_HDIM = 64
B_GROUP = B_QHEADS // B_KVHEADS
B_QW = B_QHEADS * B_HDIM
B_KVW = B_KVHEADS * B_HDIM
WINDOW = 128
EPS = 1e-6
NEG = -1e30
LANES = 128
SAMPLE_PAD = 8
Q_SLABS = B_QW // LANES
KV_SLABS = B_GROUP * B_HDIM // LANES

_SPLITS = (A_KW, A_KW, A_VW, A_VW, B_QW, B_KVW, B_KVW, B_QW, D_MODEL, D_MODEL)
_OFFS = tuple(int(sum(_SPLITS[:i])) for i in range(len(_SPLITS)))
IN_COLS = int(sum(_SPLITS))

VMEM_LIMIT = 56 * 1024 * 1024

TM_IN = 256
TM_OUT = 512
HGRN_CHUNK = 64
HGRN_SUB = 16
FUSED_ROWS = 512
PROJ_ROWS = 256
PROJ_COLS = 256
PROJ_EVERY = 4
PIPE_LAG = 7
SWA_BLOCKS = 16
SAMPLE_SEQS = 8
ROW_TILE = 128
LOG2E = math.log2(math.e)
Q_SCALE = LOG2E / math.sqrt(B_HDIM)


def _compiler_params(grid_rank):
    return pltpu.CompilerParams(dimension_semantics=("arbitrary",) * grid_rank, vmem_limit_bytes=VMEM_LIMIT)


def _sigmoid(x):
    return 1.0 / (1.0 + jnp.exp(-x))


def _silu(x):
    return x * _sigmoid(x)


def _dot_nt(a, b):
    return lax.dot_general(a, b, (((1,), (1,)), ((), ())), preferred_element_type=F32)


def _dot_tn(a, b, precision=None):
    return lax.dot_general(a, b, (((0,), (0,)), ((), ())), precision=precision, preferred_element_type=F32)


def _dot(a, b):
    return jnp.dot(a, b, preferred_element_type=F32)


def _log2(n):
    assert n & (n - 1) == 0
    return n.bit_length() - 1


def _inproj_kernel(x_ref, g_ref, lbl_ref, gn_ref, w_ref,
                   q_ref, k_ref, lf_ref, v_ref, za_ref, qb_ref, kv_ref, zb_ref, gg_ref):
    x = x_ref[...]
    ms = jnp.mean(x * x, axis=-1, keepdims=True)
    xn = (x * lax.rsqrt(ms + EPS) * g_ref[...]).astype(BF16)

    def seg(i, width=None):
        width = _SPLITS[i] if width is None else width
        return _dot(xn, w_ref[:, _OFFS[i]:_OFFS[i] + width])

    l = lbl_ref[...]
    e = jnp.exp(l - jnp.max(l, axis=0, keepdims=True))
    lb = e[0:1, :] / jnp.sum(e, axis=0, keepdims=True)

    q_ref[...] = _silu(seg(0)).astype(BF16)
    sig = _sigmoid(seg(1))
    k_ref[...] = ((1.0 - lb) * (1.0 - sig)).astype(BF16)
    lf_ref[...] = jnp.log2(lb + (1.0 - lb) * sig)
    v_ref[...] = seg(2).astype(BF16)
    za_ref[...] = (_silu(seg(3)) * gn_ref[...]).astype(BF16)
    qb = (seg(4) * Q_SCALE).astype(BF16)
    for p in range(Q_SLABS):
        qb_ref[p] = qb[:, p * LANES:(p + 1) * LANES]
    kv_ref[...] = seg(5, 2 * B_KVW)
    zb_ref[...] = _silu(seg(7)).astype(BF16)
    gg_ref[...] = _sigmoid(seg(8, 2 * D_MODEL)).astype(BF16)


def _in_projection(x2d, norm_in, lb_logits, a_gnorm, w_in_bf16, tm):
    m = x2d.shape[0]
    assert m % tm == 0
    row = lambda w: pl.BlockSpec((tm, w), lambda i: (i, 0))
    const = lambda shape: pl.BlockSpec(shape, lambda i: (0, 0), pipeline_mode=pl.Buffered(1))
    out_shapes = (
        jax.ShapeDtypeStruct((m, A_KW), BF16),
        jax.ShapeDtypeStruct((m, A_KW), BF16),
        jax.ShapeDtypeStruct((m, A_KW), F32),
        jax.ShapeDtypeStruct((m, A_VW), BF16),
        jax.ShapeDtypeStruct((m, A_VW), BF16),
        jax.ShapeDtypeStruct((Q_SLABS, m, LANES), BF16),
        jax.ShapeDtypeStruct((m, 2 * B_KVW), F32),
        jax.ShapeDtypeStruct((m, B_QW), BF16),
        jax.ShapeDtypeStruct((m, 2 * D_MODEL), BF16),
    )
    out_specs = tuple(pl.BlockSpec((Q_SLABS, tm, LANES), lambda i: (0, i, 0)) if len(s.shape) == 3
                      else row(s.shape[1]) for s in out_shapes)
    return pl.pallas_call(
        _inproj_kernel,
        grid=(m // tm,),
        in_specs=[row(D_MODEL), const((1, D_MODEL)), const(lb_logits.shape), const((1, A_VW)),
                  const((D_MODEL, IN_COLS))],
        out_specs=out_specs,
        out_shape=out_shapes,
        compiler_params=_compiler_params(1),
        name="in_projection",
    )(x2d, norm_in.reshape(1, D_MODEL), lb_logits, jnp.tile(a_gnorm, A_HEADS).reshape(1, A_VW), w_in_bf16)


def _rows(vecs, sub):
    return jnp.concatenate([jnp.broadcast_to(v, (sub, v.shape[1])) for v in vecs], axis=0)


def _tri_blocks(chunk, sub):
    r = lax.broadcasted_iota(jnp.int32, (chunk, chunk), 0)
    s = lax.broadcasted_iota(jnp.int32, (chunk, chunk), 1)
    return jnp.where((s <= r) & (((r ^ s) & ~(sub - 1)) == 0), 1.0, 0.0).astype(BF16)


def _block_cumsum_mxu(x, tri):
    hi = x.astype(BF16)
    lo = (x - hi.astype(F32)).astype(BF16)
    return _dot(tri, hi) + _dot(tri, lo)


def _hgrn_operands(q, k, ell, chunk, sub):
    _log2(sub)
    n = chunk // sub
    qf = q.astype(F32)
    kf = k.astype(F32)
    tau = [ell[(i + 1) * sub - 1:(i + 1) * sub, :] for i in range(n)]
    mu = [ell[i * sub + sub // 2 - 1:i * sub + sub // 2, :] for i in range(n)]
    beta = [jnp.zeros_like(tau[0])]
    for i in range(n):
        beta.append(beta[i] + tau[i])
    total = beta[n]

    def brow(x):
        return jnp.broadcast_to(x, (sub, x.shape[1])).astype(BF16)

    qt = (qf * jnp.exp2(ell)).astype(BF16)
    kt = (kf * jnp.exp2(_rows(tau, sub) - ell)).astype(BF16)
    mu_rows = _rows(mu, sub)
    qd = (qf * jnp.exp2(ell - mu_rows)).astype(BF16)
    kd = (kf * jnp.exp2(mu_rows - ell)).astype(BF16)
    blk = lambda x, i: x[i * sub:(i + 1) * sub]
    qhat = jnp.concatenate([blk(qt, i) * brow(jnp.exp2(beta[i])) for i in range(n)], axis=0)
    khat = jnp.concatenate([blk(kt, i) * brow(jnp.exp2(total - beta[i + 1])) for i in range(n)], axis=0)
    zero_blk = jnp.zeros((sub, A_KDIM), BF16)
    qts, kins = [], []
    for i in range(1, n):
        kin = [blk(kt, j) * brow(jnp.exp2(beta[i] - beta[j + 1])) for j in range(i)]
        kins.append(jnp.concatenate(kin + [zero_blk] * (n - i), axis=0))
        qts.append(blk(qt, i))
    return dict(qd=qd, kd=kd, qhat=qhat, khat=khat, qts=qts, kins=kins, total=total)


def _hgrn_scores(ops, st, diag_mask, state_is_kv):
    sb = st.astype(BF16)
    o_inter = _dot(ops["qhat"], sb) if state_is_kv else _dot_nt(ops["qhat"], sb)
    diag = _dot_nt(ops["qd"], ops["kd"])
    sub, chunk = ops["qhat"].shape[0] // (len(ops["qts"]) + 1), diag.shape[0]
    off = [jnp.zeros((sub, chunk), F32)] + [_dot_nt(qt, kin) for qt, kin in zip(ops["qts"], ops["kins"])]
    att = jnp.where(diag_mask, diag, jnp.concatenate(off, axis=0) if len(off) > 1 else off[0])
    return o_inter, att.astype(BF16)


def _column_broadcast(row):
    rows = lax.broadcasted_iota(jnp.int32, (16, row.shape[1]), 0)
    parts = jnp.zeros(rows.shape, F32)
    rest = row
    for i in range(3):
        part = rest.astype(BF16).astype(F32)
        parts = jnp.where(rows == i, part, parts)
        rest = rest - part
    return _dot_tn(parts.astype(BF16), jnp.where(rows < 3, 1.0, 0.0).astype(BF16))


def _hgrn_update(ops, scores, v, st, state_is_kv):
    o_inter, att = scores
    o = o_inter + _dot(att, v)
    decay = jnp.exp2(ops["total"])
    if state_is_kv:
        st_new = st * _column_broadcast(decay) + _dot_tn(ops["khat"], v)
    else:
        st_new = st * decay + _dot_tn(v, ops["khat"])
    return o, st_new


def _chunk_masks(chunk, sub):
    r = lax.broadcasted_iota(jnp.int32, (chunk, chunk), 0)
    s = lax.broadcasted_iota(jnp.int32, (chunk, chunk), 1)
    return (s <= r) & (((r ^ s) & ~(sub - 1)) == 0)


def _hgrn_finish(o, z):
    return (o * lax.rsqrt(jnp.mean(o * o, axis=-1, keepdims=True) + EPS) * z.astype(F32)).astype(BF16)


def _hgrn_sample_kernel(q_ref, k_ref, v_ref, lf_ref, z_ref, s0_ref, o_ref, sout_ref, *, t_valid, seqs):
    tp = q_ref.shape[1]
    tri = jnp.where(lax.broadcasted_iota(jnp.int32, (tp, tp), 1) < t_valid, _tri_blocks(tp, tp), 0.0).astype(BF16)
    masks = _chunk_masks(tp, tp)
    units = [(b, h, slice(h * A_KDIM, (h + 1) * A_KDIM)) for b in range(seqs) for h in range(A_HEADS)]
    ells = [_block_cumsum_mxu(lf_ref[b], tri) for b in range(seqs)]
    ops = [_hgrn_operands(q_ref[b, :, cs], k_ref[b, :, cs], ells[b][:, cs], tp, tp) for b, h, cs in units]
    scores = [_hgrn_scores(op, s0_ref[b, h], masks, True) for op, (b, h, cs) in zip(ops, units)]
    outs = []
    for op, sc, (b, h, cs) in zip(ops, scores, units):
        o, sout_ref[b, h] = _hgrn_update(op, sc, v_ref[b, :, cs], s0_ref[b, h], True)
        outs.append(o)
    for o, (b, h, cs) in zip(outs, units):
        o_ref[b, :, cs] = _hgrn_finish(o, z_ref[b, :, cs])


def _hgrn_sample(q, k, v, lf, z, state, *, t_valid, seqs):
    b, tp, _ = q.shape
    assert b % seqs == 0
    blk = lambda w: pl.BlockSpec((seqs, tp, w), lambda i: (i, 0, 0))
    sblk = pl.BlockSpec((seqs, A_HEADS, A_KDIM, A_VDIM), lambda i: (i, 0, 0, 0))
    return pl.pallas_call(
        functools.partial(_hgrn_sample_kernel, t_valid=t_valid, seqs=seqs),
        grid=(b // seqs,),
        in_specs=[blk(A_KW), blk(A_KW), blk(A_VW), blk(A_KW), blk(A_VW), sblk],
        out_specs=(blk(A_VW), sblk),
        out_shape=(jax.ShapeDtypeStruct((b, tp, A_VW), BF16),
                   jax.ShapeDtypeStruct((b, A_HEADS, A_KDIM, A_VDIM), F32)),
        compiler_params=_compiler_params(1),
        name="hgrn2_sample",
    )(q, k, v, lf, z, state)


def _alibi_slope(h):
    return 2.0 ** (-8.0 * (h + 1) / B_QHEADS)


def _lane_halves(x):
    low = lax.broadcasted_iota(jnp.int32, x.shape, 1) < B_HDIM
    swapped = pltpu.roll(x, B_HDIM, axis=1)
    both = (jnp.where(low, x, swapped), jnp.where(low, swapped, x))
    return [[jnp.where(low, h, 0.0).astype(BF16), jnp.where(low, 0.0, h).astype(BF16)] for h in both]


def _per_slab(slab, values):
    out = jnp.full(slab.shape, values[0], F32)
    for p in range(1, len(values)):
        out = jnp.where(slab == p, values[p], out)
    return out


def _sink_slot(tile):
    head = tile[:16]
    row0 = lax.broadcasted_iota(jnp.int32, head.shape, 0) == 0
    return jnp.concatenate([jnp.where(row0, jnp.zeros_like(head), head), tile[16:]], axis=0)


def _bias_table(kvh, par, sinks_ref, rows_per_slab, dist, valid, sink_in_col0):
    shift = _log2(rows_per_slab)
    heads = [kvh * B_GROUP + 2 * p + par for p in range(KV_SLABS)]
    slab = lax.broadcasted_iota(jnp.int32, dist.shape, 0) >> shift
    bias = jnp.where(valid, -_per_slab(slab, [_alibi_slope(h) * LOG2E for h in heads]) * dist.astype(F32), NEG)
    if sink_in_col0:
        col = lax.broadcasted_iota(jnp.int32, dist.shape, 1)
        bias = jnp.where(col == 0, _per_slab(slab, [sinks_ref[h] * LOG2E for h in heads]), bias)
    return bias


def _attn_scores(q4, keys):
    return [[_dot_nt(q4, ks[par]) for ks in keys] for par in range(2)]


def _attn_probs(scores, bias_fns):
    rows = scores[0][0].shape[0]
    tile = min(ROW_TILE, rows)
    out = []
    for par in range(2):
        probs = [[] for _ in bias_fns]
        for r in range(0, rows, tile):
            rs = slice(r, r + tile)
            st = [s[rs] + bf(par, rs) for s, bf in zip(scores[par], bias_fns)]
            m = None
            for s in st:
                full = [s[:, c:c + LANES] for c in range(0, s.shape[1] - LANES + 1, LANES)]
                mx = jnp.max(functools.reduce(jnp.maximum, full) if full else s, axis=-1, keepdims=True)
                m = mx if m is None else jnp.maximum(m, mx)
            for i, s in enumerate(st):
                probs[i].append(jnp.exp2(s - m).astype(BF16))
        out.append([jnp.concatenate(p, axis=0) for p in probs])
    return out


def _attn_out(probs, values):
    o = None
    for par in range(2):
        acc = None
        for p, vs in zip(probs[par], values):
            part = _dot(p, jnp.concatenate([vs[par], jnp.ones_like(vs[par])], axis=1))
            acc = part if acc is None else acc + part
        part = acc[:, :LANES] / acc[:, LANES:]
        o = part if o is None else o + part
    return o


def _swa_prompt_kernel(sinks_ref, q_ref, kvp_ref, kvc_ref, z_ref, o_ref, bias_ref, *, nblk):
    w = WINDOW
    j = pl.program_id(1)

    @pl.when((pl.program_id(0) == 0) & (j == 0))
    def _():
        shape = (KV_SLABS * w, 2 * w)
        qi = lax.broadcasted_iota(jnp.int32, shape, 0) & (w - 1)
        kj = lax.broadcasted_iota(jnp.int32, shape, 1) - w
        dist = qi - kj
        band = (dist >= 0) & (dist < w)
        for kvh in range(B_KVHEADS):
            for par in range(2):
                bias_ref[0, kvh, par] = _bias_table(kvh, par, sinks_ref, w, dist, band & (kj >= 0), True)
                bias_ref[1, kvh, par] = _bias_table(kvh, par, sinks_ref, w, dist, band, True)

    def halves(kv):
        return _lane_halves(kv[:, :B_KVW]), _lane_halves(kv[:, B_KVW:])

    prev = halves(kvp_ref[0])
    for i in range(nblk):
        rs = slice(i * w, (i + 1) * w)
        cur = halves(kvc_ref[0, rs, :])
        sel = jnp.minimum(j, 1) if i == 0 else 1
        kvs = [[[jnp.concatenate([_sink_slot(prev[x][kvh][par]), cur[x][kvh][par]], axis=0)
                 for par in range(2)] for x in range(2)] for kvh in range(B_KVHEADS)]
        scores = [_attn_scores(q_ref[kvh * KV_SLABS:(kvh + 1) * KV_SLABS, rs, :].reshape(KV_SLABS * w, LANES),
                               [kvs[kvh][0]]) for kvh in range(B_KVHEADS)]
        probs = [_attn_probs(scores[kvh], [lambda par, rows, kvh=kvh, sel=sel: bias_ref[sel, kvh, par, rows, :]])
                 for kvh in range(B_KVHEADS)]
        for kvh in range(B_KVHEADS):
            o = _attn_out(probs[kvh], [kvs[kvh][1]])
            for p in range(KV_SLABS):
                cs = slice((kvh * KV_SLABS + p) * LANES, (kvh * KV_SLABS + p + 1) * LANES)
                o_ref[0, rs, cs] = (o[p * w:(p + 1) * w] * z_ref[0, rs, cs].astype(F32)).astype(BF16)
        prev = cur


def _swa_prompt(qb, kv, zb, sinks, *, nblk):
    b, t, _ = zb.shape
    tq = nblk * WINDOW
    assert t % tq == 0
    nq = t // tq
    return pl.pallas_call(
        functools.partial(_swa_prompt_kernel, nblk=nblk),
        grid=(b, nq),
        in_specs=[pl.BlockSpec(memory_space=pltpu.SMEM),
                  pl.BlockSpec((Q_SLABS, tq, LANES), lambda i, j: (0, i * nq + j, 0)),
                  pl.BlockSpec((1, WINDOW, 2 * B_KVW), lambda i, j: (i, jnp.maximum(j * nblk - 1, 0), 0)),
                  pl.BlockSpec((1, tq, 2 * B_KVW), lambda i, j: (i, j, 0)),
                  pl.BlockSpec((1, tq, B_QW), lambda i, j: (i, j, 0))],
        out_specs=pl.BlockSpec((1, tq, B_QW), lambda i, j: (i, j, 0)),
        out_shape=jax.ShapeDtypeStruct((b, t, B_QW), BF16),
        scratch_shapes=[pltpu.VMEM((2, B_KVHEADS, 2, KV_SLABS * WINDOW, 2 * WINDOW), F32)],
        compiler_params=_compiler_params(2),
        name="swa_prompt",
    )(sinks, qb, kv, kv, zb)


def _swa_sample_kernel(sinks_ref, q_ref, kvn_ref, ck_ref, cv_ref, z_ref, o_ref, cko_ref, cvo_ref,
                       biasc_ref, biasn_ref, *, t_new, seqs):
    w = ck_ref.shape[1]
    tp = kvn_ref.shape[1]
    assert 0 < t_new < WINDOW <= w

    @pl.when(pl.program_id(0) == 0)
    def _():
        t_c = lax.broadcasted_iota(jnp.int32, (KV_SLABS * tp, w), 0) & (tp - 1)
        d_c = t_c - (lax.broadcasted_iota(jnp.int32, (KV_SLABS * tp, w), 1) - w)
        t_n = lax.broadcasted_iota(jnp.int32, (KV_SLABS * tp, tp), 0) & (tp - 1)
        s_n = lax.broadcasted_iota(jnp.int32, (KV_SLABS * tp, tp), 1)
        d_n = t_n - s_n
        for kvh in range(B_KVHEADS):
            for par in range(2):
                biasc_ref[kvh, par] = _bias_table(kvh, par, sinks_ref, tp, d_c, (d_c >= 0) & (d_c < WINDOW), True)
                biasn_ref[kvh, par] = _bias_table(kvh, par, sinks_ref, tp, d_n,
                                                  (d_n >= 0) & (d_n < WINDOW) & (s_n < t_new), False)

    keys, values = [], []
    for b in range(seqs):
        ck, cv = ck_ref[b], cv_ref[b]
        kn, vn = kvn_ref[b, :, :B_KVW], kvn_ref[b, :, B_KVW:]
        cko_ref[b, 0:w - t_new, :] = ck[t_new:w, :]
        cko_ref[b, w - t_new:w, :] = kn[0:t_new, :]
        cvo_ref[b, 0:w - t_new, :] = cv[t_new:w, :]
        cvo_ref[b, w - t_new:w, :] = vn[0:t_new, :]
        kch, vch = [[[_sink_slot(t) for t in pair] for pair in _lane_halves(x)] for x in (ck, cv)]
        knh, vnh = _lane_halves(kn), _lane_halves(vn)
        keys.append([[kch[kvh], knh[kvh]] for kvh in range(B_KVHEADS)])
        values.append([[vch[kvh], vnh[kvh]] for kvh in range(B_KVHEADS)])
    units = [(b, kvh) for b in range(seqs) for kvh in range(B_KVHEADS)]
    scores = [_attn_scores(q_ref[kvh * KV_SLABS:(kvh + 1) * KV_SLABS, b * tp:(b + 1) * tp, :]
                           .reshape(KV_SLABS * tp, LANES), keys[b][kvh]) for b, kvh in units]
    probs = [_attn_probs(s, [lambda par, rows, kvh=kvh: biasc_ref[kvh, par, rows, :],
                             lambda par, rows, kvh=kvh: biasn_ref[kvh, par, rows, :]])
             for s, (b, kvh) in zip(scores, units)]
    outs = [_attn_out(p, values[b][kvh]) for p, (b, kvh) in zip(probs, units)]
    for o, (b, kvh) in zip(outs, units):
        for p in range(KV_SLABS):
            cs = slice((kvh * KV_SLABS + p) * LANES, (kvh * KV_SLABS + p + 1) * LANES)
            o_ref[b, :, cs] = (o[p * tp:(p + 1) * tp] * z_ref[b, :, cs].astype(F32)).astype(BF16)


def _swa_sample(qb, kvn, zb, sinks, cache_k, cache_v, *, t_new, seqs):
    b, tp, _ = zb.shape
    w = cache_k.shape[1]
    assert b % seqs == 0
    blk = lambda r, c: pl.BlockSpec((seqs, r, c), lambda i: (i, 0, 0))
    tbl = lambda s: pltpu.VMEM((B_KVHEADS, 2, KV_SLABS * tp, s), F32)
    return pl.pallas_call(
        functools.partial(_swa_sample_kernel, t_new=t_new, seqs=seqs),
        grid=(b // seqs,),
        in_specs=[pl.BlockSpec(memory_space=pltpu.SMEM),
                  pl.BlockSpec((Q_SLABS, seqs * tp, LANES), lambda i: (0, i, 0)),
                  blk(tp, 2 * B_KVW), blk(w, B_KVW), blk(w, B_KVW), blk(tp, B_QW)],
        out_specs=(blk(tp, B_QW), blk(w, B_KVW), blk(w, B_KVW)),
        out_shape=(jax.ShapeDtypeStruct((b, tp, B_QW), BF16),
                   jax.ShapeDtypeStruct((b, w, B_KVW), F32),
                   jax.ShapeDtypeStruct((b, w, B_KVW), F32)),
        scratch_shapes=[tbl(w), tbl(tp)],
        compiler_params=_compiler_params(1),
        name="swa_sample",
    )(sinks, qb, kvn, cache_k, cache_v, zb)


def _merge_project(x, ha, hb, gg, wpa_ref, wpb_ref, wo_ref, gf):
    pa = _dot(ha, wpa_ref[...])
    pb = _dot(hb, wpb_ref[...])
    merged = gg[:, :D_MODEL].astype(F32) * pa + gg[:, D_MODEL:].astype(F32) * pb
    y = x + _dot(merged.astype(BF16), wo_ref[...])
    return y * lax.rsqrt(jnp.mean(y * y, axis=-1, keepdims=True) + EPS) * gf


def _outproj_kernel(x_ref, ha_ref, hb_ref, gg_ref, wpa_ref, wpb_ref, wo_ref, gf_ref, y_ref):
    y_ref[...] = _merge_project(x_ref[...], ha_ref[...], hb_ref[...], gg_ref[...], wpa_ref, wpb_ref, wo_ref,
                                gf_ref[...])


def _out_projection(x2d, ha, hb, gg, w_pa, w_pb, w_out, norm_final, tm):
    m = x2d.shape[0]
    assert m % tm == 0
    row = lambda w: pl.BlockSpec((tm, w), lambda i: (i, 0))
    const = lambda shape: pl.BlockSpec(shape, lambda i: (0, 0), pipeline_mode=pl.Buffered(1))
    return pl.pallas_call(
        _outproj_kernel,
        grid=(m // tm,),
        in_specs=[row(D_MODEL), row(A_VW), row(B_QW), row(2 * D_MODEL),
                  const((A_VW, D_MODEL)), const((B_QW, D_MODEL)), const((D_MODEL, D_MODEL)), const((1, D_MODEL))],
        out_specs=row(D_MODEL),
        out_shape=jax.ShapeDtypeStruct((m, D_MODEL), F32),
        compiler_params=_compiler_params(1),
        name="out_projection",
    )(x2d, ha, hb, gg, w_pa, w_pb, w_out, norm_final.reshape(1, D_MODEL))


def _hgrn_out_kernel(q_ref, k_ref, v_ref, lf_ref, z_ref, hb_ref, gg_ref, x_ref, wpa_ref, wpb_ref, wo_ref, gf_ref,
                     y_ref, sout_ref, st_ref, ha_ref, *, chunk, sub, group):
    tb = q_ref.shape[1]
    nt = pl.program_id(1)

    @pl.when(nt == 0)
    def _():
        st_ref[...] = jnp.zeros_like(st_ref)

    tri = _tri_blocks(chunk, sub)
    masks = _chunk_masks(chunk, sub)
    cols = [slice(h * A_KDIM, (h + 1) * A_KDIM) for h in range(A_HEADS)]
    gf = gf_ref[...]

    ells, ops, scores, outs = {}, {}, {}, {}

    def stage(s, c, h):
        rs, cs = slice(c * chunk, (c + 1) * chunk), cols[h]
        if s == 0:
            if h == 0:
                ells[c] = _block_cumsum_mxu(lf_ref[0, rs, :], tri)
        elif s == 1:
            ops[c, h] = _hgrn_operands(q_ref[0, rs, cs], k_ref[0, rs, cs], ells[c][:, cs], chunk, sub)
        elif s == 2:
            scores[c, h] = _hgrn_scores(ops[c, h], st_ref[h], masks, False)
        elif s == 3:
            outs[c, h], st_ref[h] = _hgrn_update(ops.pop((c, h)), scores.pop((c, h)), v_ref[0, rs, cs], st_ref[h],
                                                 False)
        else:
            ha_ref[rs, cs] = _hgrn_finish(outs.pop((c, h)), z_ref[0, rs, cs])

    def project(r0):
        rs = slice(r0, r0 + group)
        nb = D_MODEL // PROJ_COLS
        cb = [slice(n * PROJ_COLS, (n + 1) * PROJ_COLS) for n in range(nb)]
        merged, ys = [], []

        def branch(n):
            pa = _dot(ha_ref[rs, :], wpa_ref[:, cb[n]])
            pb = _dot(hb_ref[0, rs, :], wpb_ref[:, cb[n]])
            ga = gg_ref[0, rs, cb[n]].astype(F32)
            gb = gg_ref[0, rs, D_MODEL + n * PROJ_COLS:D_MODEL + (n + 1) * PROJ_COLS].astype(F32)
            merged.append((ga * pa + gb * pb).astype(BF16))

        def out(n):
            ys.append(x_ref[0, rs, cb[n]] + _dot(jnp.concatenate(merged, axis=1), wo_ref[:, cb[n]]))
            if n == nb - 1:
                ms = sum(jnp.sum(y * y, axis=-1, keepdims=True) for y in ys) * (1.0 / D_MODEL)
                scale = lax.rsqrt(ms + EPS)
                for m, y in enumerate(ys):
                    y_ref[0, rs, cb[m]] = y * scale * gf[:, cb[m]]

        return [functools.partial(branch, n) for n in range(nb)] + [functools.partial(out, n) for n in range(nb)]

    units = [(c, h) for c in range(tb // chunk) for h in range(A_HEADS)]
    per_group = (group // chunk) * A_HEADS
    n_stages = 5
    pending, queued_groups = [], 0
    lag = PIPE_LAG
    assert lag < A_HEADS
    for t in range(len(units) + lag * (n_stages - 1)):
        for s in range(n_stages):
            if 0 <= t - lag * s < len(units):
                stage(s, *units[t - lag * s])
        if (queued_groups + 1) * per_group <= t - lag * (n_stages - 1) + 1:
            pending += project(queued_groups * group)
            queued_groups += 1
        if pending and t % PROJ_EVERY == 0:
            pending.pop(0)()
    for step in pending:
        step()

    @pl.when(nt == pl.num_programs(1) - 1)
    def _():
        for h in range(A_HEADS):
            sout_ref[0, h] = st_ref[h].T


def _hgrn_out_prompt(q, k, v, lf, z, hb, gg, x, w_pa, w_pb, w_out, norm_final, *, tb, chunk, sub, group):
    b, t, _ = q.shape
    assert t % tb == 0 and tb % group == 0 and group % chunk == 0 and chunk % sub == 0
    blk = lambda w: pl.BlockSpec((1, tb, w), lambda i, j: (i, j, 0))
    const = lambda shape: pl.BlockSpec(shape, lambda i, j: (0, 0), pipeline_mode=pl.Buffered(1))
    return pl.pallas_call(
        functools.partial(_hgrn_out_kernel, chunk=chunk, sub=sub, group=group),
        grid=(b, t // tb),
        in_specs=[blk(A_KW), blk(A_KW), blk(A_VW), blk(A_KW), blk(A_VW), blk(B_QW), blk(2 * D_MODEL), blk(D_MODEL),
                  const((A_VW, D_MODEL)), const((B_QW, D_MODEL)), const((D_MODEL, D_MODEL)), const((1, D_MODEL))],
        out_specs=(blk(D_MODEL), pl.BlockSpec((1, A_HEADS, A_KDIM, A_VDIM), lambda i, j: (i, 0, 0, 0))),
        out_shape=(jax.ShapeDtypeStruct((b, t, D_MODEL), F32),
                   jax.ShapeDtypeStruct((b, A_HEADS, A_KDIM, A_VDIM), F32)),
        scratch_shapes=[pltpu.VMEM((A_HEADS, A_VDIM, A_KDIM), F32), pltpu.VMEM((tb, A_VW), BF16)],
        compiler_params=_compiler_params(2),
        name="hgrn2_out_prompt",
    )(q, k, v, lf, z, hb, gg, x, w_pa, w_pb, w_out, norm_final.reshape(1, D_MODEL))


def _layer(x, weights, *, hgrn_state, cache, t_valid):
    norm_in, w_in, lb_logits, a_gnorm, b_sinks, w_pa, w_pb, w_out, norm_final = weights
    b, t, d = x.shape
    x2d = x.reshape(b * t, d)
    q, k, lf, v, za, qb, kv, zb, gg = _in_projection(x2d, norm_in, lb_logits, a_gnorm, w_in, min(TM_IN, b * t))
    r3 = lambda a: a.reshape(b, t, a.shape[-1])
    kv3 = r3(kv)
    if cache is None:
        hb = _swa_prompt(qb, kv3, r3(zb), b_sinks, nblk=min(SWA_BLOCKS, t // WINDOW))
        last = min(WINDOW, t)
        ck = kv3[:, t - last:, :B_KVW]
        cv = kv3[:, t - last:, B_KVW:]
        y, s_new = _hgrn_out_prompt(r3(q), r3(k), r3(v), r3(lf), r3(za), hb, r3(gg), x, w_pa, w_pb, w_out, norm_final,
                                    tb=min(FUSED_ROWS, t), chunk=HGRN_CHUNK, sub=HGRN_SUB, group=min(PROJ_ROWS, t))
        return y, s_new, ck, cv
    seqs = min(SAMPLE_SEQS, b)
    ha, s_new = _hgrn_sample(r3(q), r3(k), r3(v), r3(lf), r3(za), hgrn_state, t_valid=t_valid, seqs=seqs)
    hb, ck, cv = _swa_sample(qb, kv3, r3(zb), b_sinks, cache[0], cache[1], t_new=t_valid, seqs=seqs)
    y = _out_projection(x2d, ha.reshape(b * t, A_VW), hb.reshape(b * t, B_QW), gg, w_pa, w_pb, w_out, norm_final,
                        min(TM_OUT, b * t))
    return y.reshape(b, t, d), s_new, ck, cv


def kernel(x_prompt, x_sample, state_hgrn, cache_k, cache_v, norm_in, w_in, lb_logits, a_gnorm, b_sinks,
           w_pa, w_pb, w_out, norm_final):
    assert state_hgrn.shape[0] == 1, "single decoder layer"
    weights = (norm_in[0], w_in[0].astype(BF16), lb_logits, a_gnorm[0], b_sinks[0],
               w_pa[0].astype(BF16), w_pb[0].astype(BF16), w_out[0].astype(BF16), norm_final)
    y_p, s_p, ck_p, cv_p = _layer(x_prompt, weights, hgrn_state=None, cache=None, t_valid=None)
    bs, ts, _ = x_sample.shape
    w = cache_k.shape[2]
    xs = jnp.pad(x_sample, ((0, 0), (0, SAMPLE_PAD - ts), (0, 0)))
    cache = (cache_k[0].reshape(bs, w, B_KVW), cache_v[0].reshape(bs, w, B_KVW))
    y_s, s_s, ck_s, cv_s = _layer(xs, weights, hgrn_state=state_hgrn[0], cache=cache, t_valid=ts)
    kvshape = lambda a: a.reshape(1, a.shape[0], a.shape[1], B_KVHEADS, B_HDIM)
    return (y_p, y_s[:, :ts], s_p[None], kvshape(ck_p), kvshape(cv_p),
            s_s[None], kvshape(ck_s), kvshape(cv_s))
```

```python
import functools
import math

import jax
import jax.numpy as jnp
from jax import lax
from jax.experimental import pallas as pl
from jax.experimental.pallas import tpu as pltpu

F32 = jnp.float32
BF16 = jnp.bfloat16

D_MODEL = 1024
A_HEADS = 8
A_KDIM = 128
A_VDIM = D_MODEL // A_HEADS
A_KW = A_HEADS * A_KDIM
A_VW = A_HEADS * A_VDIM
B_QHEADS = 16
B_KVHEADS = 2
B_HDIM = 64
B_GROUP = B_QHEADS // B_KVHEADS
B_QW = B_QHEADS * B_HDIM
B_KVW = B_KVHEADS * B_HDIM
WINDOW = 128
EPS = 1e-6
NEG = -1e30
LANES = 128
SAMPLE_PAD = 8
Q_SLABS = B_QW // LANES
KV_SLABS = B_GROUP * B_HDIM // LANES

_SPLITS = (A_KW, A_KW, A_VW, A_VW, B_QW, B_KVW, B_KVW, B_QW, D_MODEL, D_MODEL)
_OFFS = tuple(int(sum(_SPLITS[:i])) for i in range(len(_SPLITS)))
IN_COLS = int(sum(_SPLITS))

VMEM_LIMIT = 56 * 1024 * 1024

TM_IN = 256
TM_OUT = 512
HGRN_CHUNK = 64
HGRN_SUB = 16
FUSED_ROWS = 512
PROJ_ROWS = 256
PROJ_COLS = 256
PROJ_EVERY = 4
PIPE_LAG = 7
SWA_BLOCKS = 16
SAMPLE_SEQS = 16
ROW_TILE = 128
LOG2E = math.log2(math.e)
Q_SCALE = LOG2E / math.sqrt(B_HDIM)


def _compiler_params(grid_rank):
    return pltpu.CompilerParams(dimension_semantics=("arbitrary",) * grid_rank, vmem_limit_bytes=VMEM_LIMIT)


def _sigmoid(x):
    return 1.0 / (1.0 + jnp.exp(-x))


def _silu(x):
    return x * _sigmoid(x)


def _dot_nt(a, b):
    return lax.dot_general(a, b, (((1,), (1,)), ((), ())), preferred_element_type=F32)


def _dot_tn(a, b, precision=None):
    return lax.dot_general(a, b, (((0,), (0,)), ((), ())), precision=precision, preferred_element_type=F32)


def _dot(a, b):
    return jnp.dot(a, b, preferred_element_type=F32)


def _log2(n):
    assert n & (n - 1) == 0
    return n.bit_length() - 1


def _inproj_kernel(x_ref, g_ref, lbl_ref, gn_ref, w_ref,
                   q_ref, k_ref, lf_ref, v_ref, za_ref, qb_ref, kv_ref, zb_ref, gg_ref):
    x = x_ref[...]
    ms = jnp.mean(x * x, axis=-1, keepdims=True)
    xn = (x * lax.rsqrt(ms + EPS) * g_ref[...]).astype(BF16)

    def seg(i, width=None):
        width = _SPLITS[i] if width is None else width
        return _dot(xn, w_ref[:, _OFFS[i]:_OFFS[i] + width])

    l = lbl_ref[...]
    e = jnp.exp(l - jnp.max(l, axis=0, keepdims=True))
    lb = e[0:1, :] / jnp.sum(e, axis=0, keepdims=True)

    q_ref[...] = _silu(seg(0)).astype(BF16)
    sig = _sigmoid(seg(1))
    k_ref[...] = ((1.0 - lb) * (1.0 - sig)).astype(BF16)
    lf_ref[...] = jnp.log2(lb + (1.0 - lb) * sig)
    v_ref[...] = seg(2).astype(BF16)
    za_ref[...] = (_silu(seg(3)) * gn_ref[...]).astype(BF16)
    qb = (seg(4) * Q_SCALE).astype(BF16)
    for p in range(Q_SLABS):
        qb_ref[p] = qb[:, p * LANES:(p + 1) * LANES]
    kv_ref[...] = seg(5, 2 * B_KVW)
    zb_ref[...] = _silu(seg(7)).astype(BF16)
    gg_ref[...] = _sigmoid(seg(8, 2 * D_MODEL)).astype(BF16)


def _in_projection(x2d, norm_in, lb_logits, a_gnorm, w_in_bf16, tm):
    m = x2d.shape[0]
    assert m % tm == 0
    row = lambda w: pl.BlockSpec((tm, w), lambda i: (i, 0))
    const = lambda shape: pl.BlockSpec(shape, lambda i: (0, 0), pipeline_mode=pl.Buffered(1))
    out_shapes = (
        jax.ShapeDtypeStruct((m, A_KW), BF16),
        jax.ShapeDtypeStruct((m, A_KW), BF16),
        jax.ShapeDtypeStruct((m, A_KW), F32),
        jax.ShapeDtypeStruct((m, A_VW), BF16),
        jax.ShapeDtypeStruct((m, A_VW), BF16),
        jax.ShapeDtypeStruct((Q_SLABS, m, LANES), BF16),
        jax.ShapeDtypeStruct((m, 2 * B_KVW), F32),
        jax.ShapeDtypeStruct((m, B_QW), BF16),
        jax.ShapeDtypeStruct((m, 2 * D_MODEL), BF16),
    )
    out_specs = tuple(pl.BlockSpec((Q_SLABS, tm, LANES), lambda i: (0, i, 0)) if len(s.shape) == 3
                      else row(s.shape[1]) for s in out_shapes)
    return pl.pallas_call(
        _inproj_kernel,
        grid=(m // tm,),
        in_specs=[row(D_MODEL), const((1, D_MODEL)), const(lb_logits.shape), const((1, A_VW)),
                  const((D_MODEL, IN_COLS))],
        out_specs=out_specs,
        out_shape=out_shapes,
        compiler_params=_compiler_params(1),
        name="in_projection",
    )(x2d, norm_in.reshape(1, D_MODEL), lb_logits, jnp.tile(a_gnorm, A_HEADS).reshape(1, A_VW), w_in_bf16)


def _rows(vecs, sub):
    return jnp.concatenate([jnp.broadcast_to(v, (sub, v.shape[1])) for v in vecs], axis=0)


def _tri_blocks(chunk, sub):
    r = lax.broadcasted_iota(jnp.int32, (chunk, chunk), 0)
    s = lax.broadcasted_iota(jnp.int32, (chunk, chunk), 1)
    return jnp.where((s <= r) & (((r ^ s) & ~(sub - 1)) == 0), 1.0, 0.0).astype(BF16)


def _block_cumsum_mxu(x, tri):
    hi = x.astype(BF16)
    lo = (x - hi.astype(F32)).astype(BF16)
    return _dot(tri, hi) + _dot(tri, lo)


def _hgrn_operands(q, k, ell, chunk, sub):
    _log2(sub)
    n = chunk // sub
    qf = q.astype(F32)
    kf = k.astype(F32)
    tau = [ell[(i + 1) * sub - 1:(i + 1) * sub, :] for i in range(n)]
    mu = [ell[i * sub + sub // 2 - 1:i * sub + sub // 2, :] for i in range(n)]
    beta = [jnp.zeros_like(tau[0])]
    for i in range(n):
        beta.append(beta[i] + tau[i])
    total = beta[n]

    def brow(x):
        return jnp.broadcast_to(x, (sub, x.shape[1])).astype(BF16)

    qt = (qf * jnp.exp2(ell)).astype(BF16)
    kt = (kf * jnp.exp2(_rows(tau, sub) - ell)).astype(BF16)
    mu_rows = _rows(mu, sub)
    qd = (qf * jnp.exp2(ell - mu_rows)).astype(BF16)
    kd = (kf * jnp.exp2(mu_rows - ell)).astype(BF16)
    blk = lambda x, i: x[i * sub:(i + 1) * sub]
    qhat = jnp.concatenate([blk(qt, i) * brow(jnp.exp2(beta[i])) for i in range(n)], axis=0)
    khat = jnp.concatenate([blk(kt, i) * brow(jnp.exp2(total - beta[i + 1])) for i in range(n)], axis=0)
    zero_blk = jnp.zeros((sub, A_KDIM), BF16)
    qts, kins = [], []
    for i in range(1, n):
        kin = [blk(kt, j) * brow(jnp.exp2(beta[i] - beta[j + 1])) for j in range(i)]
        kins.append(jnp.concatenate(kin + [zero_blk] * (n - i), axis=0))
        qts.append(blk(qt, i))
    return dict(qd=qd, kd=kd, qhat=qhat, khat=khat, qts=qts, kins=kins, total=total)


def _hgrn_scores(ops, st, diag_mask, state_is_kv):
    sb = st.astype(BF16)
    o_inter = _dot(ops["qhat"], sb) if state_is_kv else _dot_nt(ops["qhat"], sb)
    diag = _dot_nt(ops["qd"], ops["kd"])
    sub, chunk = ops["qhat"].shape[0] // (len(ops["qts"]) + 1), diag.shape[0]
    off = [jnp.zeros((sub, chunk), F32)] + [_dot_nt(qt, kin) for qt, kin in zip(ops["qts"], ops["kins"])]
    att = jnp.where(diag_mask, diag, jnp.concatenate(off, axis=0) if len(off) > 1 else off[0])
    return o_inter, att.astype(BF16)


def _column_broadcast(row):
    rows = lax.broadcasted_iota(jnp.int32, (16, row.shape[1]), 0)
    parts = jnp.zeros(rows.shape, F32)
    rest = row
    for i in range(3):
        part = rest.astype(BF16).astype(F32)
        parts = jnp.where(rows == i, part, parts)
        rest = rest - part
    return _dot_tn(parts.astype(BF16), jnp.where(rows < 3, 1.0, 0.0).astype(BF16))


def _hgrn_update(ops, scores, v, st, state_is_kv):
    o_inter, att = scores
    o = o_inter + _dot(att, v)
    decay = jnp.exp2(ops["total"])
    if state_is_kv:
        st_new = st * _column_broadcast(decay) + _dot_tn(ops["khat"], v)
    else:
        st_new = st * decay + _dot_tn(v, ops["khat"])
    return o, st_new


def _chunk_masks(chunk, sub):
    r = lax.broadcasted_iota(jnp.int32, (chunk, chunk), 0)
    s = lax.broadcasted_iota(jnp.int32, (chunk, chunk), 1)
    return (s <= r) & (((r ^ s) & ~(sub - 1)) == 0)


def _hgrn_finish(o, z):
    return (o * lax.rsqrt(jnp.mean(o * o, axis=-1, keepdims=True) + EPS) * z.astype(F32)).astype(BF16)


def _hgrn_sample_kernel(q_ref, k_ref, v_ref, lf_ref, z_ref, s0_ref, o_ref, sout_ref, *, t_valid, seqs):
    tp = q_ref.shape[1]
    tri = jnp.where(lax.broadcasted_iota(jnp.int32, (tp, tp), 1) < t_valid, _tri_blocks(tp, tp), 0.0).astype(BF16)
    masks = _chunk_masks(tp, tp)
    units = [(b, h, slice(h * A_KDIM, (h + 1) * A_KDIM)) for b in range(seqs) for h in range(A_HEADS)]
    ells = [_block_cumsum_mxu(lf_ref[b], tri) for b in range(seqs)]
    ops = [_hgrn_operands(q_ref[b, :, cs], k_ref[b, :, cs], ells[b][:, cs], tp, tp) for b, h, cs in units]
    scores = [_hgrn_scores(op, s0_ref[b, h], masks, True) for op, (b, h, cs) in zip(ops, units)]
    outs = []
    for op, sc, (b, h, cs) in zip(ops, scores, units):
        o, sout_ref[b, h] = _hgrn_update(op, sc, v_ref[b, :, cs], s0_ref[b, h], True)
        outs.append(o)
    for o, (b, h, cs) in zip(outs, units):
        o_ref[b, :, cs] = _hgrn_finish(o, z_ref[b, :, cs])


def _hgrn_sample(q, k, v, lf, z, state, *, t_valid, seqs):
    b, tp, _ = q.shape
    assert b % seqs == 0
    blk = lambda w: pl.BlockSpec((seqs, tp, w), lambda i: (i, 0, 0))
    sblk = pl.BlockSpec((seqs, A_HEADS, A_KDIM, A_VDIM), lambda i: (i, 0, 0, 0))
    return pl.pallas_call(
        functools.partial(_hgrn_sample_kernel, t_valid=t_valid, seqs=seqs),
        grid=(b // seqs,),
        in_specs=[blk(A_KW), blk(A_KW), blk(A_VW), blk(A_KW), blk(A_VW), sblk],
        out_specs=(blk(A_VW), sblk),
        out_shape=(jax.ShapeDtypeStruct((b, tp, A_VW), BF16),
                   jax.ShapeDtypeStruct((b, A_HEADS, A_KDIM, A_VDIM), F32)),
        compiler_params=_compiler_params(1),
        name="hgrn2_sample",
    )(q, k, v, lf, z, state)


def _alibi_slope(h):
    return 2.0 ** (-8.0 * (h + 1) / B_QHEADS)


def _lane_halves(x):
    low = lax.broadcasted_iota(jnp.int32, x.shape, 1) < B_HDIM
    swapped = pltpu.roll(x, B_HDIM, axis=1)
    both = (jnp.where(low, x, swapped), jnp.where(low, swapped, x))
    return [[jnp.where(low, h, 0.0).astype(BF16), jnp.where(low, 0.0, h).astype(BF16)] for h in both]


def _per_slab(slab, values):
    out = jnp.full(slab.shape, values[0], F32)
    for p in range(1, len(values)):
        out = jnp.where(slab == p, values[p], out)
    return out


def _sink_slot(tile):
    head = tile[:16]
    row0 = lax.broadcasted_iota(jnp.int32, head.shape, 0) == 0
    return jnp.concatenate([jnp.where(row0, jnp.zeros_like(head), head), tile[16:]], axis=0)


def _bias_table(kvh, par, sinks_ref, rows_per_slab, dist, valid, sink_in_col0):
    shift = _log2(rows_per_slab)
    heads = [kvh * B_GROUP + 2 * p + par for p in range(KV_SLABS)]
    slab = lax.broadcasted_iota(jnp.int32, dist.shape, 0) >> shift
    bias = jnp.where(valid, -_per_slab(slab, [_alibi_slope(h) * LOG2E for h in heads]) * dist.astype(F32), NEG)
    if sink_in_col0:
        col = lax.broadcasted_iota(jnp.int32, dist.shape, 1)
        bias = jnp.where(col == 0, _per_slab(slab, [sinks_ref[h] * LOG2E for h in heads]), bias)
    return bias


def _attn_scores(q4, keys):
    return [[_dot_nt(q4, ks[par]) for ks in keys] for par in range(2)]


def _attn_probs(scores, bias_fns):
    rows = scores[0][0].shape[0]
    tile = min(ROW_TILE, rows)
    out = []
    for par in range(2):
        probs = [[] for _ in bias_fns]
        for r in range(0, rows, tile):
            rs = slice(r, r + tile)
            st = [s[rs] + bf(par, rs) for s, bf in zip(scores[par], bias_fns)]
            m = None
            for s in st:
                full = [s[:, c:c + LANES] for c in range(0, s.shape[1] - LANES + 1, LANES)]
                mx = jnp.max(functools.reduce(jnp.maximum, full) if full else s, axis=-1, keepdims=True)
                m = mx if m is None else jnp.maximum(m, mx)
            for i, s in enumerate(st):
                probs[i].append(jnp.exp2(s - m).astype(BF16))
        out.append([jnp.concatenate(p, axis=0) for p in probs])
    return out


def _attn_out(probs, values):
    o = None
    for par in range(2):
        acc = None
        for p, vs in zip(probs[par], values):
            part = _dot(p, jnp.concatenate([vs[par], jnp.ones_like(vs[par])], axis=1))
            acc = part if acc is None else acc + part
        part = acc[:, :LANES] / acc[:, LANES:]
        o = part if o is None else o + part
    return o


def _swa_prompt_kernel(sinks_ref, q_ref, kvp_ref, kvc_ref, z_ref, o_ref, bias_ref, *, nblk):
    w = WINDOW
    j = pl.program_id(1)

    @pl.when((pl.program_id(0) == 0) & (j == 0))
    def _():
        shape = (KV_SLABS * w, 2 * w)
        qi = lax.broadcasted_iota(jnp.int32, shape, 0) & (w - 1)
        kj = lax.broadcasted_iota(jnp.int32, shape, 1) - w
        dist = qi - kj
        band = (dist >= 0) & (dist < w)
        for kvh in range(B_KVHEADS):
            for par in range(2):
                bias_ref[0, kvh, par] = _bias_table(kvh, par, sinks_ref, w, dist, band & (kj >= 0), True)
                bias_ref[1, kvh, par] = _bias_table(kvh, par, sinks_ref, w, dist, band, True)

    def halves(kv):
        return _lane_halves(kv[:, :B_KVW]), _lane_halves(kv[:, B_KVW:])

    prev = halves(kvp_ref[0])
    for i in range(nblk):
        rs = slice(i * w, (i + 1) * w)
        cur = halves(kvc_ref[0, rs, :])
        sel = jnp.minimum(j, 1) if i == 0 else 1
        kvs = [[[jnp.concatenate([_sink_slot(prev[x][kvh][par]), cur[x][kvh][par]], axis=0)
                 for par in range(2)] for x in range(2)] for kvh in range(B_KVHEADS)]
        scores = [_attn_scores(q_ref[kvh * KV_SLABS:(kvh + 1) * KV_SLABS, rs, :].reshape(KV_SLABS * w, LANES),
                               [kvs[kvh][0]]) for kvh in range(B_KVHEADS)]
        probs = [_attn_probs(scores[kvh], [lambda par, rows, kvh=kvh, sel=sel: bias_ref[sel, kvh, par, rows, :]])
                 for kvh in range(B_KVHEADS)]
        for kvh in range(B_KVHEADS):
            o = _attn_out(probs[kvh], [kvs[kvh][1]])
            for p in range(KV_SLABS):
                cs = slice((kvh * KV_SLABS + p) * LANES, (kvh * KV_SLABS + p + 1) * LANES)
                o_ref[0, rs, cs] = (o[p * w:(p + 1) * w] * z_ref[0, rs, cs].astype(F32)).astype(BF16)
        prev = cur


def _swa_prompt(qb, kv, zb, sinks, *, nblk):
    b, t, _ = zb.shape
    tq = nblk * WINDOW
    assert t % tq == 0
    nq = t // tq
    return pl.pallas_call(
        functools.partial(_swa_prompt_kernel, nblk=nblk),
        grid=(b, nq),
        in_specs=[pl.BlockSpec(memory_space=pltpu.SMEM),
                  pl.BlockSpec((Q_SLABS, tq, LANES), lambda i, j: (0, i * nq + j, 0)),
                  pl.BlockSpec((1, WINDOW, 2 * B_KVW), lambda i, j: (i, jnp.maximum(j * nblk - 1, 0), 0)),
                  pl.BlockSpec((1, tq, 2 * B_KVW), lambda i, j: (i, j, 0)),
                  pl.BlockSpec((1, tq, B_QW), lambda i, j: (i, j, 0))],
        out_specs=pl.BlockSpec((1, tq, B_QW), lambda i, j: (i, j, 0)),
        out_shape=jax.ShapeDtypeStruct((b, t, B_QW), BF16),
        scratch_shapes=[pltpu.VMEM((2, B_KVHEADS, 2, KV_SLABS * WINDOW, 2 * WINDOW), F32)],
        compiler_params=_compiler_params(2),
        name="swa_prompt",
    )(sinks, qb, kv, kv, zb)


def _swa_sample_kernel(sinks_ref, q_ref, kvn_ref, ck_ref, cv_ref, z_ref, o_ref, cko_ref, cvo_ref,
                       biasc_ref, biasn_ref, *, t_new, seqs):
    w = ck_ref.shape[1]
    tp = kvn_ref.shape[1]
    assert 0 < t_new < WINDOW <= w

    @pl.when(pl.program_id(0) == 0)
    def _():
        t_c = lax.broadcasted_iota(jnp.int32, (KV_SLABS * tp, w), 0) & (tp - 1)
        d_c = t_c - (lax.broadcasted_iota(jnp.int32, (KV_SLABS * tp, w), 1) - w)
        t_n = lax.broadcasted_iota(jnp.int32, (KV_SLABS * tp, tp), 0) & (tp - 1)
        s_n = lax.broadcasted_iota(jnp.int32, (KV_SLABS * tp, tp), 1)
        d_n = t_n - s_n
        for kvh in range(B_KVHEADS):
            for par in range(2):
                biasc_ref[kvh, par] = _bias_table(kvh, par, sinks_ref, tp, d_c, (d_c >= 0) & (d_c < WINDOW), True)
                biasn_ref[kvh, par] = _bias_table(kvh, par, sinks_ref, tp, d_n,
                                                  (d_n >= 0) & (d_n < WINDOW) & (s_n < t_new), False)

    keys, values = [], []
    for b in range(seqs):
        ck, cv = ck_ref[b], cv_ref[b]
        kn, vn = kvn_ref[b, :, :B_KVW], kvn_ref[b, :, B_KVW:]
        cko_ref[b, 0:w - t_new, :] = ck[t_new:w, :]
        cko_ref[b, w - t_new:w, :] = kn[0:t_new, :]
        cvo_ref[b, 0:w - t_new, :] = cv[t_new:w, :]
        cvo_ref[b, w - t_new:w, :] = vn[0:t_new, :]
        kch, vch = [[[_sink_slot(t) for t in pair] for pair in _lane_halves(x)] for x in (ck, cv)]
        knh, vnh = _lane_halves(kn), _lane_halves(vn)
        keys.append([[kch[kvh], knh[kvh]] for kvh in range(B_KVHEADS)])
        values.append([[vch[kvh], vnh[kvh]] for kvh in range(B_KVHEADS)])
    units = [(b, kvh) for b in range(seqs) for kvh in range(B_KVHEADS)]
    scores = [_attn_scores(q_ref[kvh * KV_SLABS:(kvh + 1) * KV_SLABS, b * tp:(b + 1) * tp, :]
                           .reshape(KV_SLABS * tp, LANES), keys[b][kvh]) for b, kvh in units]
    probs = [_attn_probs(s, [lambda par, rows, kvh=kvh: biasc_ref[kvh, par, rows, :],
                             lambda par, rows, kvh=kvh: biasn_ref[kvh, par, rows, :]])
             for s, (b, kvh) in zip(scores, units)]
    outs = [_attn_out(p, values[b][kvh]) for p, (b, kvh) in zip(probs, units)]
    for o, (b, kvh) in zip(outs, units):
        for p in range(KV_SLABS):
            cs = slice((kvh * KV_SLABS + p) * LANES, (kvh * KV_SLABS + p + 1) * LANES)
            o_ref[b, :, cs] = (o[p * tp:(p + 1) * tp] * z_ref[b, :, cs].astype(F32)).astype(BF16)


def _swa_sample(qb, kvn, zb, sinks, cache_k, cache_v, *, t_new, seqs):
    b, tp, _ = zb.shape
    w = cache_k.shape[1]
    assert b % seqs == 0
    blk = lambda r, c: pl.BlockSpec((seqs, r, c), lambda i: (i, 0, 0))
    tbl = lambda s: pltpu.VMEM((B_KVHEADS, 2, KV_SLABS * tp, s), F32)
    return pl.pallas_call(
        functools.partial(_swa_sample_kernel, t_new=t_new, seqs=seqs),
        grid=(b // seqs,),
        in_specs=[pl.BlockSpec(memory_space=pltpu.SMEM),
                  pl.BlockSpec((Q_SLABS, seqs * tp, LANES), lambda i: (0, i, 0)),
                  blk(tp, 2 * B_KVW), blk(w, B_KVW), blk(w, B_KVW), blk(tp, B_QW)],
        out_specs=(blk(tp, B_QW), blk(w, B_KVW), blk(w, B_KVW)),
        out_shape=(jax.ShapeDtypeStruct((b, tp, B_QW), BF16),
                   jax.ShapeDtypeStruct((b, w, B_KVW), F32),
                   jax.ShapeDtypeStruct((b, w, B_KVW), F32)),
        scratch_shapes=[tbl(w), tbl(tp)],
        compiler_params=_compiler_params(1),
        name="swa_sample",
    )(sinks, qb, kvn, cache_k, cache_v, zb)


def _merge_project(x, ha, hb, gg, wpa_ref, wpb_ref, wo_ref, gf):
    pa = _dot(ha, wpa_ref[...])
    pb = _dot(hb, wpb_ref[...])
    merged = gg[:, :D_MODEL].astype(F32) * pa + gg[:, D_MODEL:].astype(F32) * pb
    y = x + _dot(merged.astype(BF16), wo_ref[...])
    return y * lax.rsqrt(jnp.mean(y * y, axis=-1, keepdims=True) + EPS) * gf


def _outproj_kernel(x_ref, ha_ref, hb_ref, gg_ref, wpa_ref, wpb_ref, wo_ref, gf_ref, y_ref):
    y_ref[...] = _merge_project(x_ref[...], ha_ref[...], hb_ref[...], gg_ref[...], wpa_ref, wpb_ref, wo_ref,
                                gf_ref[...])


def _out_projection(x2d, ha, hb, gg, w_pa, w_pb, w_out, norm_final, tm):
    m = x2d.shape[0]
    assert m % tm == 0
    row = lambda w: pl.BlockSpec((tm, w), lambda i: (i, 0))
    const = lambda shape: pl.BlockSpec(shape, lambda i: (0, 0), pipeline_mode=pl.Buffered(1))
    return pl.pallas_call(
        _outproj_kernel,
        grid=(m // tm,),
        in_specs=[row(D_MODEL), row(A_VW), row(B_QW), row(2 * D_MODEL),
                  const((A_VW, D_MODEL)), const((B_QW, D_MODEL)), const((D_MODEL, D_MODEL)), const((1, D_MODEL))],
        out_specs=row(D_MODEL),
        out_shape=jax.ShapeDtypeStruct((m, D_MODEL), F32),
        compiler_params=_compiler_params(1),
        name="out_projection",
    )(x2d, ha, hb, gg, w_pa, w_pb, w_out, norm_final.reshape(1, D_MODEL))


def _hgrn_out_kernel(q_ref, k_ref, v_ref, lf_ref, z_ref, hb_ref, gg_ref, x_ref, wpa_ref, wpb_ref, wo_ref, gf_ref,
                     y_ref, sout_ref, st_ref, ha_ref, *, chunk, sub, group):
    tb = q_ref.shape[1]
    nt = pl.program_id(1)

    @pl.when(nt == 0)
    def _():
        st_ref[...] = jnp.zeros_like(st_ref)

    tri = _tri_blocks(chunk, sub)
    masks = _chunk_masks(chunk, sub)
    cols = [slice(h * A_KDIM, (h + 1) * A_KDIM) for h in range(A_HEADS)]
    gf = gf_ref[...]

    ells, ops, scores, outs = {}, {}, {}, {}

    def stage(s, c, h):
        rs, cs = slice(c * chunk, (c + 1) * chunk), cols[h]
        if s == 0:
            if h == 0:
                ells[c] = _block_cumsum_mxu(lf_ref[0, rs, :], tri)
        elif s == 1:
            ops[c, h] = _hgrn_operands(q_ref[0, rs, cs], k_ref[0, rs, cs], ells[c][:, cs], chunk, sub)
        elif s == 2:
            scores[c, h] = _hgrn_scores(ops[c, h], st_ref[h], masks, False)
        elif s == 3:
            outs[c, h], st_ref[h] = _hgrn_update(ops.pop((c, h)), scores.pop((c, h)), v_ref[0, rs, cs], st_ref[h],
                                                 False)
        else:
            ha_ref[rs, cs] = _hgrn_finish(outs.pop((c, h)), z_ref[0, rs, cs])

    def project(r0):
        rs = slice(r0, r0 + group)
        nb = D_MODEL // PROJ_COLS
        cb = [slice(n * PROJ_COLS, (n + 1) * PROJ_COLS) for n in range(nb)]
        merged, ys = [], []

        def branch(n):
            pa = _dot(ha_ref[rs, :], wpa_ref[:, cb[n]])
            pb = _dot(hb_ref[0, rs, :], wpb_ref[:, cb[n]])
            ga = gg_ref[0, rs, cb[n]].astype(F32)
            gb = gg_ref[0, rs, D_MODEL + n * PROJ_COLS:D_MODEL + (n + 1) * PROJ_COLS].astype(F32)
            merged.append((ga * pa + gb * pb).astype(BF16))

        def out(n):
            ys.append(x_ref[0, rs, cb[n]] + _dot(jnp.concatenate(merged, axis=1), wo_ref[:, cb[n]]))
            if n == nb - 1:
                ms = sum(jnp.sum(y * y, axis=-1, keepdims=True) for y in ys) * (1.0 / D_MODEL)
                scale = lax.rsqrt(ms + EPS)
                for m, y in enumerate(ys):
                    y_ref[0, rs, cb[m]] = y * scale * gf[:, cb[m]]

        return [functools.partial(branch, n) for n in range(nb)] + [functools.partial(out, n) for n in range(nb)]

    units = [(c, h) for c in range(tb // chunk) for h in range(A_HEADS)]
    per_group = (group // chunk) * A_HEADS
    n_stages = 5
    pending, queued_groups = [], 0
    lag = PIPE_LAG
    assert lag < A_HEADS
    for t in range(len(units) + lag * (n_stages - 1)):
        for s in range(n_stages):
            if 0 <= t - lag * s < len(units):
                stage(s, *units[t - lag * s])
        if (queued_groups + 1) * per_group <= t - lag * (n_stages - 1) + 1:
            pending += project(queued_groups * group)
            queued_groups += 1
        if pending and t % PROJ_EVERY == 0:
            pending.pop(0)()
    for step in pending:
        step()

    @pl.when(nt == pl.num_programs(1) - 1)
    def _():
        for h in range(A_HEADS):
            sout_ref[0, h] = st_ref[h].T


def _hgrn_out_prompt(q, k, v, lf, z, hb, gg, x, w_pa, w_pb, w_out, norm_final, *, tb, chunk, sub, group):
    b, t, _ = q.shape
    assert t % tb == 0 and tb % group == 0 and group % chunk == 0 and chunk % sub == 0
    blk = lambda w: pl.BlockSpec((1, tb, w), lambda i, j: (i, j, 0))
    const = lambda shape: pl.BlockSpec(shape, lambda i, j: (0, 0), pipeline_mode=pl.Buffered(1))
    return pl.pallas_call(
        functools.partial(_hgrn_out_kernel, chunk=chunk, sub=sub, group=group),
        grid=(b, t // tb),
        in_specs=[blk(A_KW), blk(A_KW), blk(A_VW), blk(A_KW), blk(A_VW), blk(B_QW), blk(2 * D_MODEL), blk(D_MODEL),
                  const((A_VW, D_MODEL)), const((B_QW, D_MODEL)), const((D_MODEL, D_MODEL)), const((1, D_MODEL))],
        out_specs=(blk(D_MODEL), pl.BlockSpec((1, A_HEADS, A_KDIM, A_VDIM), lambda i, j: (i, 0, 0, 0))),
        out_shape=(jax.ShapeDtypeStruct((b, t, D_MODEL), F32),
                   jax.ShapeDtypeStruct((b, A_HEADS, A_KDIM, A_VDIM), F32)),
        scratch_shapes=[pltpu.VMEM((A_HEADS, A_VDIM, A_KDIM), F32), pltpu.VMEM((tb, A_VW), BF16)],
        compiler_params=_compiler_params(2),
        name="hgrn2_out_prompt",
    )(q, k, v, lf, z, hb, gg, x, w_pa, w_pb, w_out, norm_final.reshape(1, D_MODEL))


def _layer(x, weights, *, hgrn_state, cache, t_valid):
    norm_in, w_in, lb_logits, a_gnorm, b_sinks, w_pa, w_pb, w_out, norm_final = weights
    b, t, d = x.shape
    x2d = x.reshape(b * t, d)
    q, k, lf, v, za, qb, kv, zb, gg = _in_projection(x2d, norm_in, lb_logits, a_gnorm, w_in, min(TM_IN, b * t))
    r3 = lambda a: a.reshape(b, t, a.shape[-1])
    kv3 = r3(kv)
    if cache is None:
        hb = _swa_prompt(qb, kv3, r3(zb), b_sinks, nblk=min(SWA_BLOCKS, t // WINDOW))
        last = min(WINDOW, t)
        ck = kv3[:, t - last:, :B_KVW]
        cv = kv3[:, t - last:, B_KVW:]
        y, s_new = _hgrn_out_prompt(r3(q), r3(k), r3(v), r3(lf), r3(za), hb, r3(gg), x, w_pa, w_pb, w_out, norm_final,
                                    tb=min(FUSED_ROWS, t), chunk=HGRN_CHUNK, sub=HGRN_SUB, group=min(PROJ_ROWS, t))
        return y, s_new, ck, cv
    seqs = min(SAMPLE_SEQS, b)
    ha, s_new = _hgrn_sample(r3(q), r3(k), r3(v), r3(lf), r3(za), hgrn_state, t_valid=t_valid, seqs=seqs)
    hb, ck, cv = _swa_sample(qb, kv3, r3(zb), b_sinks, cache[0], cache[1], t_new=t_valid, seqs=seqs)
    y = _out_projection(x2d, ha.reshape(b * t, A_VW), hb.reshape(b * t, B_QW), gg, w_pa, w_pb, w_out, norm_final,
                        min(TM_OUT, b * t))
    return y.reshape(b, t, d), s_new, ck, cv


def kernel(x_prompt, x_sample, state_hgrn, cache_k, cache_v, norm_in, w_in, lb_logits, a_gnorm, b_sinks,
           w_pa, w_pb, w_out, norm_final):
    assert state_hgrn.shape[0] == 1, "single decoder layer"
    weights = (norm_in[0], w_in[0].astype(BF16), lb_logits, a_gnorm[0], b_sinks[0],
               w_pa[0].astype(BF16), w_pb[0].astype(BF16), w_out[0].astype(BF16), norm_final)
    y_p, s_p, ck_p, cv_p = _layer(x_prompt, weights, hgrn_state=None, cache=None, t_valid=None)
    bs, ts, _ = x_sample.shape
    w = cache_k.shape[2]
    xs = jnp.pad(x_sample, ((0, 0), (0, SAMPLE_PAD - ts), (0, 0)))
    cache = (cache_k[0].reshape(bs, w, B_KVW), cache_v[0].reshape(bs, w, B_KVW))
    y_s, s_s, ck_s, cv_s = _layer(xs, weights, hgrn_state=state_hgrn[0], cache=cache, t_valid=ts)
    kvshape = lambda a: a.reshape(1, a.shape[0], a.shape[1], B_KVHEADS, B_HDIM)
    return (y_p, y_s[:, :ts], s_p[None], kvshape(ck_p), kvshape(cv_p),
            s_s[None], kvshape(ck_s), kvshape(cv_s))
```

```python
import functools
import math

import jax
import jax.numpy as jnp
from jax import lax
from jax.experimental import pallas as pl
from jax.experimental.pallas import tpu as pltpu

F32 = jnp.float32
BF16 = jnp.bfloat16

D_MODEL = 1024
A_HEADS = 8
A_KDIM = 128
A_VDIM = D_MODEL // A_HEADS
A_KW = A_HEADS * A_KDIM
A_VW = A_HEADS * A_VDIM
B_QHEADS = 16
B_KVHEADS = 2
B_HDIM = 64
B_GROUP = B_QHEADS // B_KVHEADS
B_QW = B_QHEADS * B_HDIM
B_KVW = B_KVHEADS * B_HDIM
WINDOW = 128
EPS = 1e-6
NEG = -1e30
LANES = 128
SAMPLE_PAD = 8
Q_SLABS = B_QW // LANES
KV_SLABS = B_GROUP * B_HDIM // LANES

_SPLITS = (A_KW, A_KW, A_VW, A_VW, B_QW, B_KVW, B_KVW, B_QW, D_MODEL, D_MODEL)
_OFFS = tuple(int(sum(_SPLITS[:i])) for i in range(len(_SPLITS)))
IN_COLS = int(sum(_SPLITS))

VMEM_LIMIT = 56 * 1024 * 1024

TM_IN = 256
TM_OUT = 512
HGRN_CHUNK = 64
HGRN_SUB = 16
FUSED_ROWS = 512
PROJ_ROWS = 256
PROJ_COLS = 256
PROJ_EVERY = 4
INPROJ_EVERY = 8
PIPE_LAG = 7
SWA_BLOCKS = 16
SAMPLE_SEQS = 16
ROW_TILE = 128
LOG2E = math.log2(math.e)
Q_SCALE = LOG2E / math.sqrt(B_HDIM)


def _compiler_params(grid_rank):
    return pltpu.CompilerParams(dimension_semantics=("arbitrary",) * grid_rank, vmem_limit_bytes=VMEM_LIMIT)


def _sigmoid(x):
    return 1.0 / (1.0 + jnp.exp(-x))


def _silu(x):
    return x * _sigmoid(x)


def _dot_nt(a, b):
    return lax.dot_general(a, b, (((1,), (1,)), ((), ())), preferred_element_type=F32)


def _dot_tn(a, b, precision=None):
    return lax.dot_general(a, b, (((0,), (0,)), ((), ())), precision=precision, preferred_element_type=F32)


def _dot(a, b):
    return jnp.dot(a, b, preferred_element_type=F32)


def _log2(n):
    assert n & (n - 1) == 0
    return n.bit_length() - 1


def _inproj_steps(x_ref, g_ref, lbl_ref, gn_ref, w_ref,
                  q_ref, k_ref, lf_ref, v_ref, za_ref, qb_ref, kv_ref, zb_ref, gg_ref):
    x = x_ref[...]
    ms = jnp.mean(x * x, axis=-1, keepdims=True)
    xn = (x * lax.rsqrt(ms + EPS) * g_ref[...]).astype(BF16)

    def seg(i, width=None):
        width = _SPLITS[i] if width is None else width
        return _dot(xn, w_ref[:, _OFFS[i]:_OFFS[i] + width])

    l = lbl_ref[...]
    e = jnp.exp(l - jnp.max(l, axis=0, keepdims=True))
    lb = e[0:1, :] / jnp.sum(e, axis=0, keepdims=True)

    yield
    q_ref[...] = _silu(seg(0)).astype(BF16)
    yield
    sig = _sigmoid(seg(1))
    k_ref[...] = ((1.0 - lb) * (1.0 - sig)).astype(BF16)
    lf_ref[...] = jnp.log2(lb + (1.0 - lb) * sig)
    yield
    v_ref[...] = seg(2).astype(BF16)
    yield
    za_ref[...] = (_silu(seg(3)) * gn_ref[...]).astype(BF16)
    yield
    qb = (seg(4) * Q_SCALE).astype(BF16)
    for p in range(Q_SLABS):
        qb_ref[p] = qb[:, p * LANES:(p + 1) * LANES]
    yield
    kv_ref[...] = seg(5, 2 * B_KVW)
    zb_ref[...] = _silu(seg(7)).astype(BF16)
    yield
    for half in range(2):
        cs = slice(half * D_MODEL, (half + 1) * D_MODEL)
        gg_ref[:, cs] = _sigmoid(seg(8 + half)).astype(BF16)
        yield


def _inproj_kernel(*refs):
    for _ in _inproj_steps(*refs):
        pass


def _in_projection(x2d, norm_in, lb_logits, a_gnorm, w_in_bf16, tm):
    m = x2d.shape[0]
    assert m % tm == 0
    row = lambda w: pl.BlockSpec((tm, w), lambda i: (i, 0))
    const = lambda shape: pl.BlockSpec(shape, lambda i: (0, 0), pipeline_mode=pl.Buffered(1))
    out_shapes = (
        jax.ShapeDtypeStruct((m, A_KW), BF16),
        jax.ShapeDtypeStruct((m, A_KW), BF16),
        jax.ShapeDtypeStruct((m, A_KW), F32),
        jax.ShapeDtypeStruct((m, A_VW), BF16),
        jax.ShapeDtypeStruct((m, A_VW), BF16),
        jax.ShapeDtypeStruct((Q_SLABS, m, LANES), BF16),
        jax.ShapeDtypeStruct((m, 2 * B_KVW), F32),
        jax.ShapeDtypeStruct((m, B_QW), BF16),
        jax.ShapeDtypeStruct((m, 2 * D_MODEL), BF16),
    )
    out_specs = tuple(pl.BlockSpec((Q_SLABS, tm, LANES), lambda i: (0, i, 0)) if len(s.shape) == 3
                      else row(s.shape[1]) for s in out_shapes)
    return pl.pallas_call(
        _inproj_kernel,
        grid=(m // tm,),
        in_specs=[row(D_MODEL), const((1, D_MODEL)), const(lb_logits.shape), const((1, A_VW)),
                  const((D_MODEL, IN_COLS))],
        out_specs=out_specs,
        out_shape=out_shapes,
        compiler_params=_compiler_params(1),
        name="in_projection",
    )(x2d, norm_in.reshape(1, D_MODEL), lb_logits, jnp.tile(a_gnorm, A_HEADS).reshape(1, A_VW), w_in_bf16)


def _rows(vecs, sub):
    return jnp.concatenate([jnp.broadcast_to(v, (sub, v.shape[1])) for v in vecs], axis=0)


def _tri_blocks(chunk, sub):
    r = lax.broadcasted_iota(jnp.int32, (chunk, chunk), 0)
    s = lax.broadcasted_iota(jnp.int32, (chunk, chunk), 1)
    return jnp.where((s <= r) & (((r ^ s) & ~(sub - 1)) == 0), 1.0, 0.0).astype(BF16)


def _block_cumsum_mxu(x, tri):
    hi = x.astype(BF16)
    lo = (x - hi.astype(F32)).astype(BF16)
    return _dot(tri, hi) + _dot(tri, lo)


def _hgrn_operands(q, k, ell, chunk, sub):
    _log2(sub)
    n = chunk // sub
    qf = q.astype(F32)
    kf = k.astype(F32)
    tau = [ell[(i + 1) * sub - 1:(i + 1) * sub, :] for i in range(n)]
    mu = [ell[i * sub + sub // 2 - 1:i * sub + sub // 2, :] for i in range(n)]
    beta = [jnp.zeros_like(tau[0])]
    for i in range(n):
        beta.append(beta[i] + tau[i])
    total = beta[n]

    def brow(x):
        return jnp.broadcast_to(x, (sub, x.shape[1])).astype(BF16)

    qt = (qf * jnp.exp2(ell)).astype(BF16)
    kt = (kf * jnp.exp2(_rows(tau, sub) - ell)).astype(BF16)
    mu_rows = _rows(mu, sub)
    qd = (qf * jnp.exp2(ell - mu_rows)).astype(BF16)
    kd = (kf * jnp.exp2(mu_rows - ell)).astype(BF16)
    blk = lambda x, i: x[i * sub:(i + 1) * sub]
    qhat = jnp.concatenate([blk(qt, i) * brow(jnp.exp2(beta[i])) for i in range(n)], axis=0)
    khat = jnp.concatenate([blk(kt, i) * brow(jnp.exp2(total - beta[i + 1])) for i in range(n)], axis=0)
    zero_blk = jnp.zeros((sub, A_KDIM), BF16)
    qts, kins = [], []
    for i in range(1, n):
        kin = [blk(kt, j) * brow(jnp.exp2(beta[i] - beta[j + 1])) for j in range(i)]
        kins.append(jnp.concatenate(kin + [zero_blk] * (n - i), axis=0))
        qts.append(blk(qt, i))
    return dict(qd=qd, kd=kd, qhat=qhat, khat=khat, qts=qts, kins=kins, total=total)


def _hgrn_scores(ops, st, diag_mask, state_is_kv):
    sb = st.astype(BF16)
    o_inter = _dot(ops["qhat"], sb) if state_is_kv else _dot_nt(ops["qhat"], sb)
    diag = _dot_nt(ops["qd"], ops["kd"])
    sub, chunk = ops["qhat"].shape[0] // (len(ops["qts"]) + 1), diag.shape[0]
    off = [jnp.zeros((sub, chunk), F32)] + [_dot_nt(qt, kin) for qt, kin in zip(ops["qts"], ops["kins"])]
    att = jnp.where(diag_mask, diag, jnp.concatenate(off, axis=0) if len(off) > 1 else off[0])
    return o_inter, att.astype(BF16)


def _column_broadcast(row):
    rows = lax.broadcasted_iota(jnp.int32, (16, row.shape[1]), 0)
    parts = jnp.zeros(rows.shape, F32)
    rest = row
    for i in range(3):
        part = rest.astype(BF16).astype(F32)
        parts = jnp.where(rows == i, part, parts)
        rest = rest - part
    return _dot_tn(parts.astype(BF16), jnp.where(rows < 3, 1.0, 0.0).astype(BF16))


def _hgrn_update(ops, scores, v, st, state_is_kv):
    o_inter, att = scores
    o = o_inter + _dot(att, v)
    decay = jnp.exp2(ops["total"])
    if state_is_kv:
        st_new = st * _column_broadcast(decay) + _dot_tn(ops["khat"], v)
    else:
        st_new = st * decay + _dot_tn(v, ops["khat"])
    return o, st_new


def _chunk_masks(chunk, sub):
    r = lax.broadcasted_iota(jnp.int32, (chunk, chunk), 0)
    s = lax.broadcasted_iota(jnp.int32, (chunk, chunk), 1)
    return (s <= r) & (((r ^ s) & ~(sub - 1)) == 0)


def _hgrn_finish(o, z):
    return (o * lax.rsqrt(jnp.mean(o * o, axis=-1, keepdims=True) + EPS) * z.astype(F32)).astype(BF16)


def _hgrn_sample_kernel(q_ref, k_ref, v_ref, lf_ref, z_ref, s0_ref, o_ref, sout_ref, *, t_valid, seqs):
    tp = q_ref.shape[1]
    tri = jnp.where(lax.broadcasted_iota(jnp.int32, (tp, tp), 1) < t_valid, _tri_blocks(tp, tp), 0.0).astype(BF16)
    masks = _chunk_masks(tp, tp)
    units = [(b, h, slice(h * A_KDIM, (h + 1) * A_KDIM)) for b in range(seqs) for h in range(A_HEADS)]
    ells = [_block_cumsum_mxu(lf_ref[b], tri) for b in range(seqs)]
    ops = [_hgrn_operands(q_ref[b, :, cs], k_ref[b, :, cs], ells[b][:, cs], tp, tp) for b, h, cs in units]
    scores = [_hgrn_scores(op, s0_ref[b, h], masks, True) for op, (b, h, cs) in zip(ops, units)]
    outs = []
    for op, sc, (b, h, cs) in zip(ops, scores, units):
        o, sout_ref[b, h] = _hgrn_update(op, sc, v_ref[b, :, cs], s0_ref[b, h], True)
        outs.append(o)
    for o, (b, h, cs) in zip(outs, units):
        o_ref[b, :, cs] = _hgrn_finish(o, z_ref[b, :, cs])


def _hgrn_sample(q, k, v, lf, z, state, *, t_valid, seqs):
    b, tp, _ = q.shape
    assert b % seqs == 0
    blk = lambda w: pl.BlockSpec((seqs, tp, w), lambda i: (i, 0, 0))
    sblk = pl.BlockSpec((seqs, A_HEADS, A_KDIM, A_VDIM), lambda i: (i, 0, 0, 0))
    return pl.pallas_call(
        functools.partial(_hgrn_sample_kernel, t_valid=t_valid, seqs=seqs),
        grid=(b // seqs,),
        in_specs=[blk(A_KW), blk(A_KW), blk(A_VW), blk(A_KW), blk(A_VW), sblk],
        out_specs=(blk(A_VW), sblk),
        out_shape=(jax.ShapeDtypeStruct((b, tp, A_VW), BF16),
                   jax.ShapeDtypeStruct((b, A_HEADS, A_KDIM, A_VDIM), F32)),
        compiler_params=_compiler_params(1),
        name="hgrn2_sample",
    )(q, k, v, lf, z, state)


def _alibi_slope(h):
    return 2.0 ** (-8.0 * (h + 1) / B_QHEADS)


def _lane_halves(x):
    low = lax.broadcasted_iota(jnp.int32, x.shape, 1) < B_HDIM
    swapped = pltpu.roll(x, B_HDIM, axis=1)
    both = (jnp.where(low, x, swapped), jnp.where(low, swapped, x))
    return [[jnp.where(low, h, 0.0).astype(BF16), jnp.where(low, 0.0, h).astype(BF16)] for h in both]


def _per_slab(slab, values):
    out = jnp.full(slab.shape, values[0], F32)
    for p in range(1, len(values)):
        out = jnp.where(slab == p, values[p], out)
    return out


def _sink_slot(tile):
    head = tile[:16]
    row0 = lax.broadcasted_iota(jnp.int32, head.shape, 0) == 0
    return jnp.concatenate([jnp.where(row0, jnp.zeros_like(head), head), tile[16:]], axis=0)


def _bias_table(kvh, par, sinks_ref, rows_per_slab, dist, valid, sink_in_col0):
    shift = _log2(rows_per_slab)
    heads = [kvh * B_GROUP + 2 * p + par for p in range(KV_SLABS)]
    slab = lax.broadcasted_iota(jnp.int32, dist.shape, 0) >> shift
    bias = jnp.where(valid, -_per_slab(slab, [_alibi_slope(h) * LOG2E for h in heads]) * dist.astype(F32), NEG)
    if sink_in_col0:
        col = lax.broadcasted_iota(jnp.int32, dist.shape, 1)
        bias = jnp.where(col == 0, _per_slab(slab, [sinks_ref[h] * LOG2E for h in heads]), bias)
    return bias


def _attn_scores(q4, keys):
    return [[_dot_nt(q4, ks[par]) for ks in keys] for par in range(2)]


def _attn_probs(scores, bias_fns):
    rows = scores[0][0].shape[0]
    tile = min(ROW_TILE, rows)
    out = []
    for par in range(2):
        probs = [[] for _ in bias_fns]
        for r in range(0, rows, tile):
            rs = slice(r, r + tile)
            st = [s[rs] + bf(par, rs) for s, bf in zip(scores[par], bias_fns)]
            m = None
            for s in st:
                full = [s[:, c:c + LANES] for c in range(0, s.shape[1] - LANES + 1, LANES)]
                mx = jnp.max(functools.reduce(jnp.maximum, full) if full else s, axis=-1, keepdims=True)
                m = mx if m is None else jnp.maximum(m, mx)
            for i, s in enumerate(st):
                probs[i].append(jnp.exp2(s - m).astype(BF16))
        out.append([jnp.concatenate(p, axis=0) for p in probs])
    return out


def _attn_out(probs, values):
    o = None
    for par in range(2):
        acc = None
        for p, vs in zip(probs[par], values):
            part = _dot(p, jnp.concatenate([vs[par], jnp.ones_like(vs[par])], axis=1))
            acc = part if acc is None else acc + part
        part = acc[:, :LANES] / acc[:, LANES:]
        o = part if o is None else o + part
    return o


def _swa_prompt_kernel(sinks_ref, q_ref, kvp_ref, kvc_ref, z_ref, o_ref, bias_ref, *, nblk):
    w = WINDOW
    j = pl.program_id(1)

    @pl.when((pl.program_id(0) == 0) & (j == 0))
    def _():
        shape = (KV_SLABS * w, 2 * w)
        qi = lax.broadcasted_iota(jnp.int32, shape, 0) & (w - 1)
        kj = lax.broadcasted_iota(jnp.int32, shape, 1) - w
        dist = qi - kj
        band = (dist >= 0) & (dist < w)
        for kvh in range(B_KVHEADS):
            for par in range(2):
                bias_ref[0, kvh, par] = _bias_table(kvh, par, sinks_ref, w, dist, band & (kj >= 0), True)
                bias_ref[1, kvh, par] = _bias_table(kvh, par, sinks_ref, w, dist, band, True)

    def halves(kv):
        return _lane_halves(kv[:, :B_KVW]), _lane_halves(kv[:, B_KVW:])

    prev = halves(kvp_ref[0])
    for i in range(nblk):
        rs = slice(i * w, (i + 1) * w)
        cur = halves(kvc_ref[0, rs, :])
        sel = jnp.minimum(j, 1) if i == 0 else 1
        kvs = [[[jnp.concatenate([_sink_slot(prev[x][kvh][par]), cur[x][kvh][par]], axis=0)
                 for par in range(2)] for x in range(2)] for kvh in range(B_KVHEADS)]
        scores = [_attn_scores(q_ref[kvh * KV_SLABS:(kvh + 1) * KV_SLABS, rs, :].reshape(KV_SLABS * w, LANES),
                               [kvs[kvh][0]]) for kvh in range(B_KVHEADS)]
        probs = [_attn_probs(scores[kvh], [lambda par, rows, kvh=kvh, sel=sel: bias_ref[sel, kvh, par, rows, :]])
                 for kvh in range(B_KVHEADS)]
        for kvh in range(B_KVHEADS):
            o = _attn_out(probs[kvh], [kvs[kvh][1]])
            for p in range(KV_SLABS):
                cs = slice((kvh * KV_SLABS + p) * LANES, (kvh * KV_SLABS + p + 1) * LANES)
                o_ref[0, rs, cs] = (o[p * w:(p + 1) * w] * z_ref[0, rs, cs].astype(F32)).astype(BF16)
        prev = cur


def _swa_prompt(qb, kv, zb, sinks, *, nblk):
    b, t, _ = zb.shape
    tq = nblk * WINDOW
    assert t % tq == 0
    nq = t // tq
    return pl.pallas_call(
        functools.partial(_swa_prompt_kernel, nblk=nblk),
        grid=(b, nq),
        in_specs=[pl.BlockSpec(memory_space=pltpu.SMEM),
                  pl.BlockSpec((Q_SLABS, tq, LANES), lambda i, j: (0, i * nq + j, 0)),
                  pl.BlockSpec((1, WINDOW, 2 * B_KVW), lambda i, j: (i, jnp.maximum(j * nblk - 1, 0), 0)),
                  pl.BlockSpec((1, tq, 2 * B_KVW), lambda i, j: (i, j, 0)),
                  pl.BlockSpec((1, tq, B_QW), lambda i, j: (i, j, 0))],
        out_specs=pl.BlockSpec((1, tq, B_QW), lambda i, j: (i, j, 0)),
        out_shape=jax.ShapeDtypeStruct((b, t, B_QW), BF16),
        scratch_shapes=[pltpu.VMEM((2, B_KVHEADS, 2, KV_SLABS * WINDOW, 2 * WINDOW), F32)],
        compiler_params=_compiler_params(2),
        name="swa_prompt",
    )(sinks, qb, kv, kv, zb)


def _swa_sample_kernel(sinks_ref, q_ref, kvn_ref, ck_ref, cv_ref, z_ref, o_ref, cko_ref, cvo_ref,
                       biasc_ref, biasn_ref, *, t_new, seqs):
    w = ck_ref.shape[1]
    tp = kvn_ref.shape[1]
    assert 0 < t_new < WINDOW <= w

    @pl.when(pl.program_id(0) == 0)
    def _():
        t_c = lax.broadcasted_iota(jnp.int32, (KV_SLABS * tp, w), 0) & (tp - 1)
        d_c = t_c - (lax.broadcasted_iota(jnp.int32, (KV_SLABS * tp, w), 1) - w)
        t_n = lax.broadcasted_iota(jnp.int32, (KV_SLABS * tp, tp), 0) & (tp - 1)
        s_n = lax.broadcasted_iota(jnp.int32, (KV_SLABS * tp, tp), 1)
        d_n = t_n - s_n
        for kvh in range(B_KVHEADS):
            for par in range(2):
                biasc_ref[kvh, par] = _bias_table(kvh, par, sinks_ref, tp, d_c, (d_c >= 0) & (d_c < WINDOW), True)
                biasn_ref[kvh, par] = _bias_table(kvh, par, sinks_ref, tp, d_n,
                                                  (d_n >= 0) & (d_n < WINDOW) & (s_n < t_new), False)

    keys, values = [], []
    for b in range(seqs):
        ck, cv = ck_ref[b], cv_ref[b]
        kn, vn = kvn_ref[b, :, :B_KVW], kvn_ref[b, :, B_KVW:]
        cko_ref[b, 0:w - t_new, :] = ck[t_new:w, :]
        cko_ref[b, w - t_new:w, :] = kn[0:t_new, :]
        cvo_ref[b, 0:w - t_new, :] = cv[t_new:w, :]
        cvo_ref[b, w - t_new:w, :] = vn[0:t_new, :]
        kch, vch = [[[_sink_slot(t) for t in pair] for pair in _lane_halves(x)] for x in (ck, cv)]
        knh, vnh = _lane_halves(kn), _lane_halves(vn)
        keys.append([[kch[kvh], knh[kvh]] for kvh in range(B_KVHEADS)])
        values.append([[vch[kvh], vnh[kvh]] for kvh in range(B_KVHEADS)])
    units = [(b, kvh) for b in range(seqs) for kvh in range(B_KVHEADS)]
    scores = [_attn_scores(q_ref[kvh * KV_SLABS:(kvh + 1) * KV_SLABS, b * tp:(b + 1) * tp, :]
                           .reshape(KV_SLABS * tp, LANES), keys[b][kvh]) for b, kvh in units]
    probs = [_attn_probs(s, [lambda par, rows, kvh=kvh: biasc_ref[kvh, par, rows, :],
                             lambda par, rows, kvh=kvh: biasn_ref[kvh, par, rows, :]])
             for s, (b, kvh) in zip(scores, units)]
    outs = [_attn_out(p, values[b][kvh]) for p, (b, kvh) in zip(probs, units)]
    for o, (b, kvh) in zip(outs, units):
        for p in range(KV_SLABS):
            cs = slice((kvh * KV_SLABS + p) * LANES, (kvh * KV_SLABS + p + 1) * LANES)
            o_ref[b, :, cs] = (o[p * tp:(p + 1) * tp] * z_ref[b, :, cs].astype(F32)).astype(BF16)


def _swa_sample(qb, kvn, zb, sinks, cache_k, cache_v, *, t_new, seqs):
    b, tp, _ = zb.shape
    w = cache_k.shape[1]
    assert b % seqs == 0
    blk = lambda r, c: pl.BlockSpec((seqs, r, c), lambda i: (i, 0, 0))
    tbl = lambda s: pltpu.VMEM((B_KVHEADS, 2, KV_SLABS * tp, s), F32)
    return pl.pallas_call(
        functools.partial(_swa_sample_kernel, t_new=t_new, seqs=seqs),
        grid=(b // seqs,),
        in_specs=[pl.BlockSpec(memory_space=pltpu.SMEM),
                  pl.BlockSpec((Q_SLABS, seqs * tp, LANES), lambda i: (0, i, 0)),
                  blk(tp, 2 * B_KVW), blk(w, B_KVW), blk(w, B_KVW), blk(tp, B_QW)],
        out_specs=(blk(tp, B_QW), blk(w, B_KVW), blk(w, B_KVW)),
        out_shape=(jax.ShapeDtypeStruct((b, tp, B_QW), BF16),
                   jax.ShapeDtypeStruct((b, w, B_KVW), F32),
                   jax.ShapeDtypeStruct((b, w, B_KVW), F32)),
        scratch_shapes=[tbl(w), tbl(tp)],
        compiler_params=_compiler_params(1),
        name="swa_sample",
    )(sinks, qb, kvn, cache_k, cache_v, zb)


def _merge_project(x, ha, hb, gg, wpa_ref, wpb_ref, wo_ref, gf):
    pa = _dot(ha, wpa_ref[...])
    pb = _dot(hb, wpb_ref[...])
    merged = gg[:, :D_MODEL].astype(F32) * pa + gg[:, D_MODEL:].astype(F32) * pb
    y = x + _dot(merged.astype(BF16), wo_ref[...])
    return y * lax.rsqrt(jnp.mean(y * y, axis=-1, keepdims=True) + EPS) * gf


def _outproj_kernel(x_ref, ha_ref, hb_ref, gg_ref, wpa_ref, wpb_ref, wo_ref, gf_ref, y_ref):
    y_ref[...] = _merge_project(x_ref[...], ha_ref[...], hb_ref[...], gg_ref[...], wpa_ref, wpb_ref, wo_ref,
                                gf_ref[...])


def _out_projection(x2d, ha, hb, gg, w_pa, w_pb, w_out, norm_final, tm):
    m = x2d.shape[0]
    assert m % tm == 0
    row = lambda w: pl.BlockSpec((tm, w), lambda i: (i, 0))
    const = lambda shape: pl.BlockSpec(shape, lambda i: (0, 0), pipeline_mode=pl.Buffered(1))
    return pl.pallas_call(
        _outproj_kernel,
        grid=(m // tm,),
        in_specs=[row(D_MODEL), row(A_VW), row(B_QW), row(2 * D_MODEL),
                  const((A_VW, D_MODEL)), const((B_QW, D_MODEL)), const((D_MODEL, D_MODEL)), const((1, D_MODEL))],
        out_specs=row(D_MODEL),
        out_shape=jax.ShapeDtypeStruct((m, D_MODEL), F32),
        compiler_params=_compiler_params(1),
        name="out_projection",
    )(x2d, ha, hb, gg, w_pa, w_pb, w_out, norm_final.reshape(1, D_MODEL))


def _hgrn_out_kernel(q_ref, k_ref, v_ref, lf_ref, z_ref, hb_ref, gg_ref, x_ref, wpa_ref, wpb_ref, wo_ref, gf_ref,
                     y_ref, sout_ref, st_ref, ha_ref, *, chunk, sub, group):
    tb = q_ref.shape[1]
    nt = pl.program_id(1)

    @pl.when(nt == 0)
    def _():
        st_ref[...] = jnp.zeros_like(st_ref)

    tri = _tri_blocks(chunk, sub)
    masks = _chunk_masks(chunk, sub)
    cols = [slice(h * A_KDIM, (h + 1) * A_KDIM) for h in range(A_HEADS)]
    gf = gf_ref[...]

    ells, ops, scores, outs = {}, {}, {}, {}

    def stage(s, c, h):
        rs, cs = slice(c * chunk, (c + 1) * chunk), cols[h]
        if s == 0:
            if h == 0:
                ells[c] = _block_cumsum_mxu(lf_ref[0, rs, :], tri)
        elif s == 1:
            ops[c, h] = _hgrn_operands(q_ref[0, rs, cs], k_ref[0, rs, cs], ells[c][:, cs], chunk, sub)
        elif s == 2:
            scores[c, h] = _hgrn_scores(ops[c, h], st_ref[h], masks, False)
        elif s == 3:
            outs[c, h], st_ref[h] = _hgrn_update(ops.pop((c, h)), scores.pop((c, h)), v_ref[0, rs, cs], st_ref[h],
                                                 False)
        else:
            ha_ref[rs, cs] = _hgrn_finish(outs.pop((c, h)), z_ref[0, rs, cs])

    def project(r0):
        rs = slice(r0, r0 + group)
        nb = D_MODEL // PROJ_COLS
        cb = [slice(n * PROJ_COLS, (n + 1) * PROJ_COLS) for n in range(nb)]
        merged, ys = [], []

        def branch(n):
            pa = _dot(ha_ref[rs, :], wpa_ref[:, cb[n]])
            pb = _dot(hb_ref[0, rs, :], wpb_ref[:, cb[n]])
            ga = gg_ref[0, rs, cb[n]].astype(F32)
            gb = gg_ref[0, rs, D_MODEL + n * PROJ_COLS:D_MODEL + (n + 1) * PROJ_COLS].astype(F32)
            merged.append((ga * pa + gb * pb).astype(BF16))

        def out(n):
            ys.append(x_ref[0, rs, cb[n]] + _dot(jnp.concatenate(merged, axis=1), wo_ref[:, cb[n]]))
            if n == nb - 1:
                ms = sum(jnp.sum(y * y, axis=-1, keepdims=True) for y in ys) * (1.0 / D_MODEL)
                scale = lax.rsqrt(ms + EPS)
                for m, y in enumerate(ys):
                    y_ref[0, rs, cb[m]] = y * scale * gf[:, cb[m]]

        return [functools.partial(branch, n) for n in range(nb)] + [functools.partial(out, n) for n in range(nb)]

    units = [(c, h) for c in range(tb // chunk) for h in range(A_HEADS)]
    per_group = (group // chunk) * A_HEADS
    n_stages = 5
    pending, queued_groups = [], 0
    lag = PIPE_LAG
    assert lag < A_HEADS
    for t in range(len(units) + lag * (n_stages - 1)):
        for s in range(n_stages):
            if 0 <= t - lag * s < len(units):
                stage(s, *units[t - lag * s])
        if (queued_groups + 1) * per_group <= t - lag * (n_stages - 1) + 1:
            pending += project(queued_groups * group)
            queued_groups += 1
        if pending and t % PROJ_EVERY == 0:
            pending.pop(0)()
    for step in pending:
        step()

    @pl.when(nt == pl.num_programs(1) - 1)
    def _():
        for h in range(A_HEADS):
            sout_ref[0, h] = st_ref[h].T


def _hgrn_out_prompt(q, k, v, lf, z, hb, gg, x, w_pa, w_pb, w_out, norm_final, *, tb, chunk, sub, group):
    b, t, _ = q.shape
    assert t % tb == 0 and tb % group == 0 and group % chunk == 0 and chunk % sub == 0
    blk = lambda w: pl.BlockSpec((1, tb, w), lambda i, j: (i, j, 0))
    const = lambda shape: pl.BlockSpec(shape, lambda i, j: (0, 0), pipeline_mode=pl.Buffered(1))
    return pl.pallas_call(
        functools.partial(_hgrn_out_kernel, chunk=chunk, sub=sub, group=group),
        grid=(b, t // tb),
        in_specs=[blk(A_KW), blk(A_KW), blk(A_VW), blk(A_KW), blk(A_VW), blk(B_QW), blk(2 * D_MODEL), blk(D_MODEL),
                  const((A_VW, D_MODEL)), const((B_QW, D_MODEL)), const((D_MODEL, D_MODEL)), const((1, D_MODEL))],
        out_specs=(blk(D_MODEL), pl.BlockSpec((1, A_HEADS, A_KDIM, A_VDIM), lambda i, j: (i, 0, 0, 0))),
        out_shape=(jax.ShapeDtypeStruct((b, t, D_MODEL), F32),
                   jax.ShapeDtypeStruct((b, A_HEADS, A_KDIM, A_VDIM), F32)),
        scratch_shapes=[pltpu.VMEM((A_HEADS, A_VDIM, A_KDIM), F32), pltpu.VMEM((tb, A_VW), BF16)],
        compiler_params=_compiler_params(2),
        name="hgrn2_out_prompt",
    )(q, k, v, lf, z, hb, gg, x, w_pa, w_pb, w_out, norm_final.reshape(1, D_MODEL))


def _inproj_hgrn_kernel(x_ref, g_ref, lbl_ref, gn_ref, w_ref, ha_ref, sout_ref, qb_ref, kv_ref, zb_ref, gg_ref,
                        st_ref, q_s, k_s, lf_s, v_s, z_s, *, chunk, sub, blocks_per_seq):
    j = pl.program_id(0)
    rows = x_ref.shape[0]
    slot = j % 2
    prev = 1 - slot
    seq_pos = (j + blocks_per_seq - 1) % blocks_per_seq

    @pl.when(j == 0)
    def _():
        for s in (q_s, k_s, lf_s, v_s, z_s):
            s[...] = jnp.zeros_like(s)

    @pl.when(seq_pos == 0)
    def _():
        st_ref[...] = jnp.zeros_like(st_ref)

    in_steps = _inproj_steps(x_ref, g_ref, lbl_ref, gn_ref, w_ref, q_s.at[slot], k_s.at[slot], lf_s.at[slot],
                             v_s.at[slot], z_s.at[slot], qb_ref, kv_ref, zb_ref, gg_ref)
    tri = _tri_blocks(chunk, sub)
    masks = _chunk_masks(chunk, sub)
    cols = [slice(h * A_KDIM, (h + 1) * A_KDIM) for h in range(A_HEADS)]
    ells, ops, scores, outs = {}, {}, {}, {}

    def stage(s, c, h):
        rs, cs = slice(c * chunk, (c + 1) * chunk), cols[h]
        if s == 0:
            if h == 0:
                ells[c] = _block_cumsum_mxu(lf_s[prev, rs, :], tri)
        elif s == 1:
            ops[c, h] = _hgrn_operands(q_s[prev, rs, cs], k_s[prev, rs, cs], ells[c][:, cs], chunk, sub)
        elif s == 2:
            scores[c, h] = _hgrn_scores(ops[c, h], st_ref[h], masks, False)
        elif s == 3:
            outs[c, h], st_ref[h] = _hgrn_update(ops.pop((c, h)), scores.pop((c, h)), v_s[prev, rs, cs], st_ref[h],
                                                 False)
        else:
            ha_ref[rs, cs] = _hgrn_finish(outs.pop((c, h)), z_s[prev, rs, cs])

    units = [(c, h) for c in range(rows // chunk) for h in range(A_HEADS)]
    n_stages = 5
    lag = PIPE_LAG
    assert lag < A_HEADS
    for t in range(len(units) + lag * (n_stages - 1)):
        for s in range(n_stages):
            if 0 <= t - lag * s < len(units):
                stage(s, *units[t - lag * s])
        if t % INPROJ_EVERY == 0:
            next(in_steps, None)
    for _ in in_steps:
        pass

    @pl.when(seq_pos == blocks_per_seq - 1)
    def _():
        for h in range(A_HEADS):
            sout_ref[0, h] = st_ref[h].T


def _inproj_hgrn_prompt(x2d, norm_in, lb_logits, a_gnorm, w_in_bf16, *, rows, seq_len, chunk, sub):
    m = x2d.shape[0]
    assert m % seq_len == 0 and seq_len % rows == 0 and rows % chunk == 0 and chunk % sub == 0
    n, bps = m // rows, seq_len // rows
    cur = lambda j: jnp.minimum(j, n - 1)
    prv = lambda j: jnp.maximum(j - 1, 0)
    const = lambda shape: pl.BlockSpec(shape, lambda j: (0, 0), pipeline_mode=pl.Buffered(1))
    row_cur = lambda w: pl.BlockSpec((rows, w), lambda j: (cur(j), 0))
    out_shapes = (
        jax.ShapeDtypeStruct((m, A_VW), BF16),
        jax.ShapeDtypeStruct((m // seq_len, A_HEADS, A_KDIM, A_VDIM), F32),
        jax.ShapeDtypeStruct((Q_SLABS, m, LANES), BF16),
        jax.ShapeDtypeStruct((m, 2 * B_KVW), F32),
        jax.ShapeDtypeStruct((m, B_QW), BF16),
        jax.ShapeDtypeStruct((m, 2 * D_MODEL), BF16),
    )
    out_specs = (
        pl.BlockSpec((rows, A_VW), lambda j: (prv(j), 0)),
        pl.BlockSpec((1, A_HEADS, A_KDIM, A_VDIM), lambda j: (prv(j) // bps, 0, 0, 0)),
        pl.BlockSpec((Q_SLABS, rows, LANES), lambda j: (0, cur(j), 0)),
        row_cur(2 * B_KVW), row_cur(B_QW), row_cur(2 * D_MODEL),
    )
    carry = lambda w, dt: pltpu.VMEM((2, rows, w), dt)
    return pl.pallas_call(
        functools.partial(_inproj_hgrn_kernel, chunk=chunk, sub=sub, blocks_per_seq=bps),
        grid=(n + 1,),
        in_specs=[row_cur(D_MODEL), const((1, D_MODEL)), const(lb_logits.shape), const((1, A_VW)),
                  const((D_MODEL, IN_COLS))],
        out_specs=out_specs,
        out_shape=out_shapes,
        scratch_shapes=[pltpu.VMEM((A_HEADS, A_VDIM, A_KDIM), F32), carry(A_KW, BF16), carry(A_KW, BF16),
                        carry(A_KW, F32), carry(A_VW, BF16), carry(A_VW, BF16)],
        compiler_params=_compiler_params(1),
        name="inproj_hgrn2_prompt",
    )(x2d, norm_in.reshape(1, D_MODEL), lb_logits, jnp.tile(a_gnorm, A_HEADS).reshape(1, A_VW), w_in_bf16)


def _layer(x, weights, *, hgrn_state, cache, t_valid):
    norm_in, w_in, lb_logits, a_gnorm, b_sinks, w_pa, w_pb, w_out, norm_final = weights
    b, t, d = x.shape
    x2d = x.reshape(b * t, d)
    r3 = lambda a: a.reshape(b, t, a.shape[-1])
    if cache is None:
        ha, s_new, qb, kv, zb, gg = _inproj_hgrn_prompt(x2d, norm_in, lb_logits, a_gnorm, w_in,
                                                        rows=min(FUSED_ROWS, t), seq_len=t, chunk=HGRN_CHUNK,
                                                        sub=HGRN_SUB)
        kv3 = r3(kv)
        hb = _swa_prompt(qb, kv3, r3(zb), b_sinks, nblk=min(SWA_BLOCKS, t // WINDOW))
        last = min(WINDOW, t)
        ck = kv3[:, t - last:, :B_KVW]
        cv = kv3[:, t - last:, B_KVW:]
        y = _out_projection(x2d, ha, hb.reshape(b * t, B_QW), gg, w_pa, w_pb, w_out, norm_final, min(TM_OUT, b * t))
        return y.reshape(b, t, d), s_new, ck, cv
    q, k, lf, v, za, qb, kv, zb, gg = _in_projection(x2d, norm_in, lb_logits, a_gnorm, w_in, min(TM_IN, b * t))
    kv3 = r3(kv)
    seqs = min(SAMPLE_SEQS, b)
    ha, s_new = _hgrn_sample(r3(q), r3(k), r3(v), r3(lf), r3(za), hgrn_state, t_valid=t_valid, seqs=seqs)
    hb, ck, cv = _swa_sample(qb, kv3, r3(zb), b_sinks, cache[0], cache[1], t_new=t_valid, seqs=seqs)
    y = _out_projection(x2d, ha.reshape(b * t, A_VW), hb.reshape(b * t, B_QW), gg, w_pa, w_pb, w_out, norm_final,
                        min(TM_OUT, b * t))
    return y.reshape(b, t, d), s_new, ck, cv


def kernel(x_prompt, x_sample, state_hgrn, cache_k, cache_v, norm_in, w_in, lb_logits, a_gnorm, b_sinks,
           w_pa, w_pb, w_out, norm_final):
    assert state_hgrn.shape[0] == 1, "single decoder layer"
    weights = (norm_in[0], w_in[0].astype(BF16), lb_logits, a_gnorm[0], b_sinks[0],
               w_pa[0].astype(BF16), w_pb[0].astype(BF16), w_out[0].astype(BF16), norm_final)
    y_p, s_p, ck_p, cv_p = _layer(x_prompt, weights, hgrn_state=None, cache=None, t_valid=None)
    bs, ts, _ = x_sample.shape
    w = cache_k.shape[2]
    xs = jnp.pad(x_sample, ((0, 0), (0, SAMPLE_PAD - ts), (0, 0)))
    cache = (cache_k[0].reshape(bs, w, B_KVW), cache_v[0].reshape(bs, w, B_KVW))
    y_s, s_s, ck_s, cv_s = _layer(xs, weights, hgrn_state=state_hgrn[0], cache=cache, t_valid=ts)
    kvshape = lambda a: a.reshape(1, a.shape[0], a.shape[1], B_KVHEADS, B_HDIM)
    return (y_p, y_s[:, :ts], s_p[None], kvshape(ck_p), kvshape(cv_p),
            s_s[None], kvshape(ck_s), kvshape(cv_s))
```

```python
import functools
import math

import jax
import jax.numpy as jnp
from jax import lax
from jax.experimental import pallas as pl
from jax.experimental.pallas import tpu as pltpu

F32 = jnp.float32
BF16 = jnp.bfloat16

D_MODEL = 1024
A_HEADS = 8
A_KDIM = 128
A_VDIM = D_MODEL // A_HEADS
A_KW = A_HEADS * A_KDIM
A_VW = A_HEADS * A_VDIM
B_QHEADS = 16
B_KVHEADS = 2
B_HDIM = 64
B_GROUP = B_QHEADS // B_KVHEADS
B_QW = B_QHEADS * B_HDIM
B_KVW = B_KVHEADS * B_HDIM
WINDOW = 128
EPS = 1e-6
NEG = -1e30
LANES = 128
SAMPLE_PAD = 8
Q_SLABS = B_QW // LANES
KV_SLABS = B_GROUP * B_HDIM // LANES

_SPLITS = (A_KW, A_KW, A_VW, A_VW, B_QW, B_KVW, B_KVW, B_QW, D_MODEL, D_MODEL)
_OFFS = tuple(int(sum(_SPLITS[:i])) for i in range(len(_SPLITS)))
IN_COLS = int(sum(_SPLITS))

VMEM_LIMIT = 56 * 1024 * 1024

TM_IN = 256
TM_OUT = 512
HGRN_CHUNK = 64
HGRN_SUB = 16
FUSED_ROWS = 512
PROJ_ROWS = 256
PROJ_COLS = 256
PROJ_EVERY = 4
PIPE_LAG = 7
SWA_BLOCKS = 16
SAMPLE_SEQS = 16
ROW_TILE = 128
LOG2E = math.log2(math.e)
Q_SCALE = LOG2E / math.sqrt(B_HDIM)


def _compiler_params(grid_rank):
    return pltpu.CompilerParams(dimension_semantics=("arbitrary",) * grid_rank, vmem_limit_bytes=VMEM_LIMIT)


def _sigmoid(x):
    return 1.0 / (1.0 + jnp.exp(-x))


def _silu(x):
    return x * _sigmoid(x)


def _dot_nt(a, b):
    return lax.dot_general(a, b, (((1,), (1,)), ((), ())), preferred_element_type=F32)


def _dot_tn(a, b, precision=None):
    return lax.dot_general(a, b, (((0,), (0,)), ((), ())), precision=precision, preferred_element_type=F32)


def _dot(a, b):
    return jnp.dot(a, b, preferred_element_type=F32)


def _log2(n):
    assert n & (n - 1) == 0
    return n.bit_length() - 1


def _inproj_kernel(x_ref, g_ref, lbl_ref, gn_ref, w_ref,
                   q_ref, k_ref, lf_ref, v_ref, za_ref, qb_ref, kv_ref, zb_ref, gg_ref):
    x = x_ref[...]
    ms = jnp.mean(x * x, axis=-1, keepdims=True)
    xn = (x * lax.rsqrt(ms + EPS) * g_ref[...]).astype(BF16)

    def seg(i, width=None):
        width = _SPLITS[i] if width is None else width
        return _dot(xn, w_ref[:, _OFFS[i]:_OFFS[i] + width])

    l = lbl_ref[...]
    e = jnp.exp(l - jnp.max(l, axis=0, keepdims=True))
    lb = e[0:1, :] / jnp.sum(e, axis=0, keepdims=True)

    q_ref[...] = _silu(seg(0)).astype(BF16)
    sig = _sigmoid(seg(1))
    k_ref[...] = ((1.0 - lb) * (1.0 - sig)).astype(BF16)
    lf_ref[...] = jnp.log2(lb + (1.0 - lb) * sig)
    v_ref[...] = seg(2).astype(BF16)
    za_ref[...] = (_silu(seg(3)) * gn_ref[...]).astype(BF16)
    qb = (seg(4) * Q_SCALE).astype(BF16)
    for p in range(Q_SLABS):
        qb_ref[p] = qb[:, p * LANES:(p + 1) * LANES]
    kv_ref[...] = seg(5, 2 * B_KVW)
    zb_ref[...] = _silu(seg(7)).astype(BF16)
    gg_ref[...] = _sigmoid(seg(8, 2 * D_MODEL)).astype(BF16)


def _in_projection(x2d, norm_in, lb_logits, a_gnorm, w_in_bf16, tm):
    m = x2d.shape[0]
    assert m % tm == 0
    row = lambda w: pl.BlockSpec((tm, w), lambda i: (i, 0))
    const = lambda shape: pl.BlockSpec(shape, lambda i: (0, 0), pipeline_mode=pl.Buffered(1))
    out_shapes = (
        jax.ShapeDtypeStruct((m, A_KW), BF16),
        jax.ShapeDtypeStruct((m, A_KW), BF16),
        jax.ShapeDtypeStruct((m, A_KW), F32),
        jax.ShapeDtypeStruct((m, A_VW), BF16),
        jax.ShapeDtypeStruct((m, A_VW), BF16),
        jax.ShapeDtypeStruct((Q_SLABS, m, LANES), BF16),
        jax.ShapeDtypeStruct((m, 2 * B_KVW), F32),
        jax.ShapeDtypeStruct((m, B_QW), BF16),
        jax.ShapeDtypeStruct((m, 2 * D_MODEL), BF16),
    )
    out_specs = tuple(pl.BlockSpec((Q_SLABS, tm, LANES), lambda i: (0, i, 0)) if len(s.shape) == 3
                      else row(s.shape[1]) for s in out_shapes)
    return pl.pallas_call(
        _inproj_kernel,
        grid=(m // tm,),
        in_specs=[row(D_MODEL), const((1, D_MODEL)), const(lb_logits.shape), const((1, A_VW)),
                  const((D_MODEL, IN_COLS))],
        out_specs=out_specs,
        out_shape=out_shapes,
        compiler_params=_compiler_params(1),
        name="in_projection",
    )(x2d, norm_in.reshape(1, D_MODEL), lb_logits, jnp.tile(a_gnorm, A_HEADS).reshape(1, A_VW), w_in_bf16)


def _rows(vecs, sub):
    return jnp.concatenate([jnp.broadcast_to(v, (sub, v.shape[1])) for v in vecs], axis=0)


def _tri_blocks(chunk, sub):
    r = lax.broadcasted_iota(jnp.int32, (chunk, chunk), 0)
    s = lax.broadcasted_iota(jnp.int32, (chunk, chunk), 1)
    return jnp.where((s <= r) & (((r ^ s) & ~(sub - 1)) == 0), 1.0, 0.0).astype(BF16)


def _block_cumsum_mxu(x, tri):
    hi = x.astype(BF16)
    lo = (x - hi.astype(F32)).astype(BF16)
    return _dot(tri, hi) + _dot(tri, lo)


def _hgrn_operands(q, k, ell, chunk, sub):
    _log2(sub)
    n = chunk // sub
    qf = q.astype(F32)
    kf = k.astype(F32)
    tau = [ell[(i + 1) * sub - 1:(i + 1) * sub, :] for i in range(n)]
    mu = [ell[i * sub + sub // 2 - 1:i * sub + sub // 2, :] for i in range(n)]
    beta = [jnp.zeros_like(tau[0])]
    for i in range(n):
        beta.append(beta[i] + tau[i])
    total = beta[n]

    def brow(x):
        return jnp.broadcast_to(x, (sub, x.shape[1])).astype(BF16)

    qt = (qf * jnp.exp2(ell)).astype(BF16)
    kt = (kf * jnp.exp2(_rows(tau, sub) - ell)).astype(BF16)
    mu_rows = _rows(mu, sub)
    qd = (qf * jnp.exp2(ell - mu_rows)).astype(BF16)
    kd = (kf * jnp.exp2(mu_rows - ell)).astype(BF16)
    blk = lambda x, i: x[i * sub:(i + 1) * sub]
    qhat = jnp.concatenate([blk(qt, i) * brow(jnp.exp2(beta[i])) for i in range(n)], axis=0)
    khat = jnp.concatenate([blk(kt, i) * brow(jnp.exp2(total - beta[i + 1])) for i in range(n)], axis=0)
    zero_blk = jnp.zeros((sub, A_KDIM), BF16)
    qts, kins = [], []
    for i in range(1, n):
        kin = [blk(kt, j) * brow(jnp.exp2(beta[i] - beta[j + 1])) for j in range(i)]
        kins.append(jnp.concatenate(kin + [zero_blk] * (n - i), axis=0))
        qts.append(blk(qt, i))
    return dict(qd=qd, kd=kd, qhat=qhat, khat=khat, qts=qts, kins=kins, total=total)


def _hgrn_scores(ops, st, diag_mask, state_is_kv):
    sb = st.astype(BF16)
    o_inter = _dot(ops["qhat"], sb) if state_is_kv else _dot_nt(ops["qhat"], sb)
    diag = _dot_nt(ops["qd"], ops["kd"])
    sub, chunk = ops["qhat"].shape[0] // (len(ops["qts"]) + 1), diag.shape[0]
    off = [jnp.zeros((sub, chunk), F32)] + [_dot_nt(qt, kin) for qt, kin in zip(ops["qts"], ops["kins"])]
    att = jnp.where(diag_mask, diag, jnp.concatenate(off, axis=0) if len(off) > 1 else off[0])
    return o_inter, att.astype(BF16)


def _column_broadcast(row):
    rows = lax.broadcasted_iota(jnp.int32, (16, row.shape[1]), 0)
    parts = jnp.zeros(rows.shape, F32)
    rest = row
    for i in range(3):
        part = rest.astype(BF16).astype(F32)
        parts = jnp.where(rows == i, part, parts)
        rest = rest - part
    return _dot_tn(parts.astype(BF16), jnp.where(rows < 3, 1.0, 0.0).astype(BF16))


def _hgrn_update(ops, scores, v, st, state_is_kv):
    o_inter, att = scores
    o = o_inter + _dot(att, v)
    decay = jnp.exp2(ops["total"])
    if state_is_kv:
        st_new = st * _column_broadcast(decay) + _dot_tn(ops["khat"], v)
    else:
        st_new = st * decay + _dot_tn(v, ops["khat"])
    return o, st_new


def _chunk_masks(chunk, sub):
    r = lax.broadcasted_iota(jnp.int32, (chunk, chunk), 0)
    s = lax.broadcasted_iota(jnp.int32, (chunk, chunk), 1)
    return (s <= r) & (((r ^ s) & ~(sub - 1)) == 0)


def _hgrn_finish(o, z):
    return (o * lax.rsqrt(jnp.mean(o * o, axis=-1, keepdims=True) + EPS) * z.astype(F32)).astype(BF16)


def _hgrn_sample_kernel(q_ref, k_ref, v_ref, lf_ref, z_ref, s0_ref, o_ref, sout_ref, *, t_valid, seqs):
    tp = q_ref.shape[1]
    tri = jnp.where(lax.broadcasted_iota(jnp.int32, (tp, tp), 1) < t_valid, _tri_blocks(tp, tp), 0.0).astype(BF16)
    masks = _chunk_masks(tp, tp)
    units = [(b, h, slice(h * A_KDIM, (h + 1) * A_KDIM)) for b in range(seqs) for h in range(A_HEADS)]
    ells = [_block_cumsum_mxu(lf_ref[b], tri) for b in range(seqs)]
    ops = [_hgrn_operands(q_ref[b, :, cs], k_ref[b, :, cs], ells[b][:, cs], tp, tp) for b, h, cs in units]
    scores = [_hgrn_scores(op, s0_ref[b, h], masks, True) for op, (b, h, cs) in zip(ops, units)]
    outs = []
    for op, sc, (b, h, cs) in zip(ops, scores, units):
        o, sout_ref[b, h] = _hgrn_update(op, sc, v_ref[b, :, cs], s0_ref[b, h], True)
        outs.append(o)
    for o, (b, h, cs) in zip(outs, units):
        o_ref[b, :, cs] = _hgrn_finish(o, z_ref[b, :, cs])


def _hgrn_sample(q, k, v, lf, z, state, *, t_valid, seqs):
    b, tp, _ = q.shape
    assert b % seqs == 0
    blk = lambda w: pl.BlockSpec((seqs, tp, w), lambda i: (i, 0, 0))
    sblk = pl.BlockSpec((seqs, A_HEADS, A_KDIM, A_VDIM), lambda i: (i, 0, 0, 0))
    return pl.pallas_call(
        functools.partial(_hgrn_sample_kernel, t_valid=t_valid, seqs=seqs),
        grid=(b // seqs,),
        in_specs=[blk(A_KW), blk(A_KW), blk(A_VW), blk(A_KW), blk(A_VW), sblk],
        out_specs=(blk(A_VW), sblk),
        out_shape=(jax.ShapeDtypeStruct((b, tp, A_VW), BF16),
                   jax.ShapeDtypeStruct((b, A_HEADS, A_KDIM, A_VDIM), F32)),
        compiler_params=_compiler_params(1),
        name="hgrn2_sample",
    )(q, k, v, lf, z, state)


def _alibi_slope(h):
    return 2.0 ** (-8.0 * (h + 1) / B_QHEADS)


def _lane_halves(x):
    low = lax.broadcasted_iota(jnp.int32, x.shape, 1) < B_HDIM
    swapped = pltpu.roll(x, B_HDIM, axis=1)
    both = (jnp.where(low, x, swapped), jnp.where(low, swapped, x))
    return [[jnp.where(low, h, 0.0).astype(BF16), jnp.where(low, 0.0, h).astype(BF16)] for h in both]


def _per_slab(slab, values):
    out = jnp.full(slab.shape, values[0], F32)
    for p in range(1, len(values)):
        out = jnp.where(slab == p, values[p], out)
    return out


def _sink_slot(tile):
    head = tile[:16]
    row0 = lax.broadcasted_iota(jnp.int32, head.shape, 0) == 0
    return jnp.concatenate([jnp.where(row0, jnp.zeros_like(head), head), tile[16:]], axis=0)


def _bias_table(kvh, par, sinks_ref, rows_per_slab, dist, valid, sink_in_col0):
    shift = _log2(rows_per_slab)
    heads = [kvh * B_GROUP + 2 * p + par for p in range(KV_SLABS)]
    slab = lax.broadcasted_iota(jnp.int32, dist.shape, 0) >> shift
    bias = jnp.where(valid, -_per_slab(slab, [_alibi_slope(h) * LOG2E for h in heads]) * dist.astype(F32), NEG)
    if sink_in_col0:
        col = lax.broadcasted_iota(jnp.int32, dist.shape, 1)
        bias = jnp.where(col == 0, _per_slab(slab, [sinks_ref[h] * LOG2E for h in heads]), bias)
    return bias


def _attn_scores(q4, keys):
    return [[_dot_nt(q4, ks[par]) for ks in keys] for par in range(2)]


def _attn_probs(scores, bias_fns):
    rows = scores[0][0].shape[0]
    tile = min(ROW_TILE, rows)
    out = []
    for par in range(2):
        probs = [[] for _ in bias_fns]
        for r in range(0, rows, tile):
            rs = slice(r, r + tile)
            st = [s[rs] + bf(par, rs) for s, bf in zip(scores[par], bias_fns)]
            m = None
            for s in st:
                full = [s[:, c:c + LANES] for c in range(0, s.shape[1] - LANES + 1, LANES)]
                mx = jnp.max(functools.reduce(jnp.maximum, full) if full else s, axis=-1, keepdims=True)
                m = mx if m is None else jnp.maximum(m, mx)
            for i, s in enumerate(st):
                probs[i].append(jnp.exp2(s - m).astype(BF16))
        out.append([jnp.concatenate(p, axis=0) for p in probs])
    return out


def _attn_out(probs, values):
    o = None
    for par in range(2):
        acc = None
        for p, vs in zip(probs[par], values):
            part = _dot(p, jnp.concatenate([vs[par], jnp.ones_like(vs[par])], axis=1))
            acc = part if acc is None else acc + part
        part = acc[:, :LANES] / acc[:, LANES:]
        o = part if o is None else o + part
    return o


def _swa_prompt_kernel(sinks_ref, q_ref, kvp_ref, kvc_ref, z_ref, o_ref, bias_ref, *, nblk):
    w = WINDOW
    j = pl.program_id(1)

    @pl.when((pl.program_id(0) == 0) & (j == 0))
    def _():
        shape = (KV_SLABS * w, 2 * w)
        qi = lax.broadcasted_iota(jnp.int32, shape, 0) & (w - 1)
        kj = lax.broadcasted_iota(jnp.int32, shape, 1) - w
        dist = qi - kj
        band = (dist >= 0) & (dist < w)
        for kvh in range(B_KVHEADS):
            for par in range(2):
                bias_ref[0, kvh, par] = _bias_table(kvh, par, sinks_ref, w, dist, band & (kj >= 0), True)
                bias_ref[1, kvh, par] = _bias_table(kvh, par, sinks_ref, w, dist, band, True)

    def halves(kv):
        return _lane_halves(kv[:, :B_KVW]), _lane_halves(kv[:, B_KVW:])

    prev = halves(kvp_ref[0])
    for i in range(nblk):
        rs = slice(i * w, (i + 1) * w)
        cur = halves(kvc_ref[0, rs, :])
        sel = jnp.minimum(j, 1) if i == 0 else 1
        kvs = [[[jnp.concatenate([_sink_slot(prev[x][kvh][par]), cur[x][kvh][par]], axis=0)
                 for par in range(2)] for x in range(2)] for kvh in range(B_KVHEADS)]
        scores = [_attn_scores(q_ref[kvh * KV_SLABS:(kvh + 1) * KV_SLABS, rs, :].reshape(KV_SLABS * w, LANES),
                               [kvs[kvh][0]]) for kvh in range(B_KVHEADS)]
        probs = [_attn_probs(scores[kvh], [lambda par, rows, kvh=kvh, sel=sel: bias_ref[sel, kvh, par, rows, :]])
                 for kvh in range(B_KVHEADS)]
        for kvh in range(B_KVHEADS):
            o = _attn_out(probs[kvh], [kvs[kvh][1]])
            for p in range(KV_SLABS):
                cs = slice((kvh * KV_SLABS + p) * LANES, (kvh * KV_SLABS + p + 1) * LANES)
                o_ref[0, rs, cs] = (o[p * w:(p + 1) * w] * z_ref[0, rs, cs].astype(F32)).astype(BF16)
        prev = cur


def _swa_prompt(qb, kv, zb, sinks, *, nblk):
    b, t, _ = zb.shape
    tq = nblk * WINDOW
    assert t % tq == 0
    nq = t // tq
    return pl.pallas_call(
        functools.partial(_swa_prompt_kernel, nblk=nblk),
        grid=(b, nq),
        in_specs=[pl.BlockSpec(memory_space=pltpu.SMEM),
                  pl.BlockSpec((Q_SLABS, tq, LANES), lambda i, j: (0, i * nq + j, 0)),
                  pl.BlockSpec((1, WINDOW, 2 * B_KVW), lambda i, j: (i, jnp.maximum(j * nblk - 1, 0), 0)),
                  pl.BlockSpec((1, tq, 2 * B_KVW), lambda i, j: (i, j, 0)),
                  pl.BlockSpec((1, tq, B_QW), lambda i, j: (i, j, 0))],
        out_specs=pl.BlockSpec((1, tq, B_QW), lambda i, j: (i, j, 0)),
        out_shape=jax.ShapeDtypeStruct((b, t, B_QW), BF16),
        scratch_shapes=[pltpu.VMEM((2, B_KVHEADS, 2, KV_SLABS * WINDOW, 2 * WINDOW), F32)],
        compiler_params=_compiler_params(2),
        name="swa_prompt",
    )(sinks, qb, kv, kv, zb)


def _swa_sample_kernel(sinks_ref, q_ref, kvn_ref, ck_ref, cv_ref, z_ref, o_ref, cko_ref, cvo_ref,
                       biasc_ref, biasn_ref, *, t_new, seqs):
    w = ck_ref.shape[1]
    tp = kvn_ref.shape[1]
    assert 0 < t_new < WINDOW <= w

    @pl.when(pl.program_id(0) == 0)
    def _():
        t_c = lax.broadcasted_iota(jnp.int32, (KV_SLABS * tp, w), 0) & (tp - 1)
        d_c = t_c - (lax.broadcasted_iota(jnp.int32, (KV_SLABS * tp, w), 1) - w)
        t_n = lax.broadcasted_iota(jnp.int32, (KV_SLABS * tp, tp), 0) & (tp - 1)
        s_n = lax.broadcasted_iota(jnp.int32, (KV_SLABS * tp, tp), 1)
        d_n = t_n - s_n
        for kvh in range(B_KVHEADS):
            for par in range(2):
                biasc_ref[kvh, par] = _bias_table(kvh, par, sinks_ref, tp, d_c, (d_c >= 0) & (d_c < WINDOW), True)
                biasn_ref[kvh, par] = _bias_table(kvh, par, sinks_ref, tp, d_n,
                                                  (d_n >= 0) & (d_n < WINDOW) & (s_n < t_new), False)

    keys, values = [], []
    for b in range(seqs):
        ck, cv = ck_ref[b], cv_ref[b]
        kn, vn = kvn_ref[b, :, :B_KVW], kvn_ref[b, :, B_KVW:]
        cko_ref[b, 0:w - t_new, :] = ck[t_new:w, :]
        cko_ref[b, w - t_new:w, :] = kn[0:t_new, :]
        cvo_ref[b, 0:w - t_new, :] = cv[t_new:w, :]
        cvo_ref[b, w - t_new:w, :] = vn[0:t_new, :]
        kch, vch = [[[_sink_slot(t) for t in pair] for pair in _lane_halves(x)] for x in (ck, cv)]
        knh, vnh = _lane_halves(kn), _lane_halves(vn)
        keys.append([[kch[kvh], knh[kvh]] for kvh in range(B_KVHEADS)])
        values.append([[vch[kvh], vnh[kvh]] for kvh in range(B_KVHEADS)])
    units = [(b, kvh) for b in range(seqs) for kvh in range(B_KVHEADS)]
    scores = [_attn_scores(q_ref[kvh * KV_SLABS:(kvh + 1) * KV_SLABS, b * tp:(b + 1) * tp, :]
                           .reshape(KV_SLABS * tp, LANES), keys[b][kvh]) for b, kvh in units]
    probs = [_attn_probs(s, [lambda par, rows, kvh=kvh: biasc_ref[kvh, par, rows, :],
                             lambda par, rows, kvh=kvh: biasn_ref[kvh, par, rows, :]])
             for s, (b, kvh) in zip(scores, units)]
    outs = [_attn_out(p, values[b][kvh]) for p, (b, kvh) in zip(probs, units)]
    for o, (b, kvh) in zip(outs, units):
        for p in range(KV_SLABS):
            cs = slice((kvh * KV_SLABS + p) * LANES, (kvh * KV_SLABS + p + 1) * LANES)
            o_ref[b, :, cs] = (o[p * tp:(p + 1) * tp] * z_ref[b, :, cs].astype(F32)).astype(BF16)


def _swa_sample(qb, kvn, zb, sinks, cache_k, cache_v, *, t_new, seqs):
    b, tp, _ = zb.shape
    w = cache_k.shape[1]
    assert b % seqs == 0
    blk = lambda r, c: pl.BlockSpec((seqs, r, c), lambda i: (i, 0, 0))
    tbl = lambda s: pltpu.VMEM((B_KVHEADS, 2, KV_SLABS * tp, s), F32)
    return pl.pallas_call(
        functools.partial(_swa_sample_kernel, t_new=t_new, seqs=seqs),
        grid=(b // seqs,),
        in_specs=[pl.BlockSpec(memory_space=pltpu.SMEM),
                  pl.BlockSpec((Q_SLABS, seqs * tp, LANES), lambda i: (0, i, 0)),
                  blk(tp, 2 * B_KVW), blk(w, B_KVW), blk(w, B_KVW), blk(tp, B_QW)],
        out_specs=(blk(tp, B_QW), blk(w, B_KVW), blk(w, B_KVW)),
        out_shape=(jax.ShapeDtypeStruct((b, tp, B_QW), BF16),
                   jax.ShapeDtypeStruct((b, w, B_KVW), F32),
                   jax.ShapeDtypeStruct((b, w, B_KVW), F32)),
        scratch_shapes=[tbl(w), tbl(tp)],
        compiler_params=_compiler_params(1),
        name="swa_sample",
    )(sinks, qb, kvn, cache_k, cache_v, zb)


def _sample_kernel(sinks_ref, q_ref, k_ref, v_ref, lf_ref, z_ref, s0_ref, qb_ref, kvn_ref, ck_ref, cv_ref, zb_ref,
                   ha_ref, sout_ref, hb_ref, cko_ref, cvo_ref, biasc_ref, biasn_ref, *, t_valid, seqs):
    _swa_sample_kernel(sinks_ref, qb_ref, kvn_ref, ck_ref, cv_ref, zb_ref, hb_ref, cko_ref, cvo_ref,
                       biasc_ref, biasn_ref, t_new=t_valid, seqs=seqs)
    _hgrn_sample_kernel(q_ref, k_ref, v_ref, lf_ref, z_ref, s0_ref, ha_ref, sout_ref, t_valid=t_valid, seqs=seqs)


def _sample_branches(q, k, v, lf, z, state, qb, kvn, zb, sinks, cache_k, cache_v, *, t_valid, seqs):
    b, tp, _ = q.shape
    w = cache_k.shape[1]
    assert b % seqs == 0
    blk = lambda r, c: pl.BlockSpec((seqs, r, c), lambda i: (i, 0, 0))
    sblk = pl.BlockSpec((seqs, A_HEADS, A_KDIM, A_VDIM), lambda i: (i, 0, 0, 0))
    tbl = lambda s: pltpu.VMEM((B_KVHEADS, 2, KV_SLABS * tp, s), F32)
    return pl.pallas_call(
        functools.partial(_sample_kernel, t_valid=t_valid, seqs=seqs),
        grid=(b // seqs,),
        in_specs=[pl.BlockSpec(memory_space=pltpu.SMEM),
                  blk(tp, A_KW), blk(tp, A_KW), blk(tp, A_VW), blk(tp, A_KW), blk(tp, A_VW), sblk,
                  pl.BlockSpec((Q_SLABS, seqs * tp, LANES), lambda i: (0, i, 0)),
                  blk(tp, 2 * B_KVW), blk(w, B_KVW), blk(w, B_KVW), blk(tp, B_QW)],
        out_specs=(blk(tp, A_VW), sblk, blk(tp, B_QW), blk(w, B_KVW), blk(w, B_KVW)),
        out_shape=(jax.ShapeDtypeStruct((b, tp, A_VW), BF16),
                   jax.ShapeDtypeStruct((b, A_HEADS, A_KDIM, A_VDIM), F32),
                   jax.ShapeDtypeStruct((b, tp, B_QW), BF16),
                   jax.ShapeDtypeStruct((b, w, B_KVW), F32),
                   jax.ShapeDtypeStruct((b, w, B_KVW), F32)),
        scratch_shapes=[tbl(w), tbl(tp)],
        compiler_params=_compiler_params(1),
        name="sample_branches",
    )(sinks, q, k, v, lf, z, state, qb, kvn, cache_k, cache_v, zb)


def _merge_project(x, ha, hb, gg, wpa_ref, wpb_ref, wo_ref, gf):
    pa = _dot(ha, wpa_ref[...])
    pb = _dot(hb, wpb_ref[...])
    merged = gg[:, :D_MODEL].astype(F32) * pa + gg[:, D_MODEL:].astype(F32) * pb
    y = x + _dot(merged.astype(BF16), wo_ref[...])
    return y * lax.rsqrt(jnp.mean(y * y, axis=-1, keepdims=True) + EPS) * gf


def _outproj_kernel(x_ref, ha_ref, hb_ref, gg_ref, wpa_ref, wpb_ref, wo_ref, gf_ref, y_ref):
    y_ref[...] = _merge_project(x_ref[...], ha_ref[...], hb_ref[...], gg_ref[...], wpa_ref, wpb_ref, wo_ref,
                                gf_ref[...])


def _out_projection(x2d, ha, hb, gg, w_pa, w_pb, w_out, norm_final, tm):
    m = x2d.shape[0]
    assert m % tm == 0
    row = lambda w: pl.BlockSpec((tm, w), lambda i: (i, 0))
    const = lambda shape: pl.BlockSpec(shape, lambda i: (0, 0), pipeline_mode=pl.Buffered(1))
    return pl.pallas_call(
        _outproj_kernel,
        grid=(m // tm,),
        in_specs=[row(D_MODEL), row(A_VW), row(B_QW), row(2 * D_MODEL),
                  const((A_VW, D_MODEL)), const((B_QW, D_MODEL)), const((D_MODEL, D_MODEL)), const((1, D_MODEL))],
        out_specs=row(D_MODEL),
        out_shape=jax.ShapeDtypeStruct((m, D_MODEL), F32),
        compiler_params=_compiler_params(1),
        name="out_projection",
    )(x2d, ha, hb, gg, w_pa, w_pb, w_out, norm_final.reshape(1, D_MODEL))


def _hgrn_out_kernel(q_ref, k_ref, v_ref, lf_ref, z_ref, hb_ref, gg_ref, x_ref, wpa_ref, wpb_ref, wo_ref, gf_ref,
                     y_ref, sout_ref, st_ref, ha_ref, *, chunk, sub, group):
    tb = q_ref.shape[1]
    nt = pl.program_id(1)

    @pl.when(nt == 0)
    def _():
        st_ref[...] = jnp.zeros_like(st_ref)

    tri = _tri_blocks(chunk, sub)
    masks = _chunk_masks(chunk, sub)
    cols = [slice(h * A_KDIM, (h + 1) * A_KDIM) for h in range(A_HEADS)]
    gf = gf_ref[...]

    ells, ops, scores, outs = {}, {}, {}, {}

    def stage(s, c, h):
        rs, cs = slice(c * chunk, (c + 1) * chunk), cols[h]
        if s == 0:
            if h == 0:
                ells[c] = _block_cumsum_mxu(lf_ref[0, rs, :], tri)
        elif s == 1:
            ops[c, h] = _hgrn_operands(q_ref[0, rs, cs], k_ref[0, rs, cs], ells[c][:, cs], chunk, sub)
        elif s == 2:
            scores[c, h] = _hgrn_scores(ops[c, h], st_ref[h], masks, False)
        elif s == 3:
            outs[c, h], st_ref[h] = _hgrn_update(ops.pop((c, h)), scores.pop((c, h)), v_ref[0, rs, cs], st_ref[h],
                                                 False)
        else:
            ha_ref[rs, cs] = _hgrn_finish(outs.pop((c, h)), z_ref[0, rs, cs])

    def project(r0):
        rs = slice(r0, r0 + group)
        nb = D_MODEL // PROJ_COLS
        cb = [slice(n * PROJ_COLS, (n + 1) * PROJ_COLS) for n in range(nb)]
        merged, ys = [], []

        def branch(n):
            pa = _dot(ha_ref[rs, :], wpa_ref[:, cb[n]])
            pb = _dot(hb_ref[0, rs, :], wpb_ref[:, cb[n]])
            ga = gg_ref[0, rs, cb[n]].astype(F32)
            gb = gg_ref[0, rs, D_MODEL + n * PROJ_COLS:D_MODEL + (n + 1) * PROJ_COLS].astype(F32)
            merged.append((ga * pa + gb * pb).astype(BF16))

        def out(n):
            ys.append(x_ref[0, rs, cb[n]] + _dot(jnp.concatenate(merged, axis=1), wo_ref[:, cb[n]]))
            if n == nb - 1:
                ms = sum(jnp.sum(y * y, axis=-1, keepdims=True) for y in ys) * (1.0 / D_MODEL)
                scale = lax.rsqrt(ms + EPS)
                for m, y in enumerate(ys):
                    y_ref[0, rs, cb[m]] = y * scale * gf[:, cb[m]]

        return [functools.partial(branch, n) for n in range(nb)] + [functools.partial(out, n) for n in range(nb)]

    units = [(c, h) for c in range(tb // chunk) for h in range(A_HEADS)]
    per_group = (group // chunk) * A_HEADS
    n_stages = 5
    pending, queued_groups = [], 0
    lag = PIPE_LAG
    assert lag < A_HEADS
    for t in range(len(units) + lag * (n_stages - 1)):
        for s in range(n_stages):
            if 0 <= t - lag * s < len(units):
                stage(s, *units[t - lag * s])
        if (queued_groups + 1) * per_group <= t - lag * (n_stages - 1) + 1:
            pending += project(queued_groups * group)
            queued_groups += 1
        if pending and t % PROJ_EVERY == 0:
            pending.pop(0)()
    for step in pending:
        step()

    @pl.when(nt == pl.num_programs(1) - 1)
    def _():
        for h in range(A_HEADS):
            sout_ref[0, h] = st_ref[h].T


def _hgrn_out_prompt(q, k, v, lf, z, hb, gg, x, w_pa, w_pb, w_out, norm_final, *, tb, chunk, sub, group):
    b, t, _ = q.shape
    assert t % tb == 0 and tb % group == 0 and group % chunk == 0 and chunk % sub == 0
    blk = lambda w: pl.BlockSpec((1, tb, w), lambda i, j: (i, j, 0))
    const = lambda shape: pl.BlockSpec(shape, lambda i, j: (0, 0), pipeline_mode=pl.Buffered(1))
    return pl.pallas_call(
        functools.partial(_hgrn_out_kernel, chunk=chunk, sub=sub, group=group),
        grid=(b, t // tb),
        in_specs=[blk(A_KW), blk(A_KW), blk(A_VW), blk(A_KW), blk(A_VW), blk(B_QW), blk(2 * D_MODEL), blk(D_MODEL),
                  const((A_VW, D_MODEL)), const((B_QW, D_MODEL)), const((D_MODEL, D_MODEL)), const((1, D_MODEL))],
        out_specs=(blk(D_MODEL), pl.BlockSpec((1, A_HEADS, A_KDIM, A_VDIM), lambda i, j: (i, 0, 0, 0))),
        out_shape=(jax.ShapeDtypeStruct((b, t, D_MODEL), F32),
                   jax.ShapeDtypeStruct((b, A_HEADS, A_KDIM, A_VDIM), F32)),
        scratch_shapes=[pltpu.VMEM((A_HEADS, A_VDIM, A_KDIM), F32), pltpu.VMEM((tb, A_VW), BF16)],
        compiler_params=_compiler_params(2),
        name="hgrn2_out_prompt",
    )(q, k, v, lf, z, hb, gg, x, w_pa, w_pb, w_out, norm_final.reshape(1, D_MODEL))


def _layer(x, weights, *, hgrn_state, cache, t_valid):
    norm_in, w_in, lb_logits, a_gnorm, b_sinks, w_pa, w_pb, w_out, norm_final = weights
    b, t, d = x.shape
    x2d = x.reshape(b * t, d)
    q, k, lf, v, za, qb, kv, zb, gg = _in_projection(x2d, norm_in, lb_logits, a_gnorm, w_in, min(TM_IN, b * t))
    r3 = lambda a: a.reshape(b, t, a.shape[-1])
    kv3 = r3(kv)
    if cache is None:
        hb = _swa_prompt(qb, kv3, r3(zb), b_sinks, nblk=min(SWA_BLOCKS, t // WINDOW))
        last = min(WINDOW, t)
        ck = kv3[:, t - last:, :B_KVW]
        cv = kv3[:, t - last:, B_KVW:]
        y, s_new = _hgrn_out_prompt(r3(q), r3(k), r3(v), r3(lf), r3(za), hb, r3(gg), x, w_pa, w_pb, w_out, norm_final,
                                    tb=min(FUSED_ROWS, t), chunk=HGRN_CHUNK, sub=HGRN_SUB, group=min(PROJ_ROWS, t))
        return y, s_new, ck, cv
    seqs = min(SAMPLE_SEQS, b)
    ha, s_new, hb, ck, cv = _sample_branches(r3(q), r3(k), r3(v), r3(lf), r3(za), hgrn_state, qb, kv3, r3(zb), b_sinks,
                                             cache[0], cache[1], t_valid=t_valid, seqs=seqs)
    y = _out_projection(x2d, ha.reshape(b * t, A_VW), hb.reshape(b * t, B_QW), gg, w_pa, w_pb, w_out, norm_final,
                        min(TM_OUT, b * t))
    return y.reshape(b, t, d), s_new, ck, cv


def kernel(x_prompt, x_sample, state_hgrn, cache_k, cache_v, norm_in, w_in, lb_logits, a_gnorm, b_sinks,
           w_pa, w_pb, w_out, norm_final):
    assert state_hgrn.shape[0] == 1, "single decoder layer"
    weights = (norm_in[0], w_in[0].astype(BF16), lb_logits, a_gnorm[0], b_sinks[0],
               w_pa[0].astype(BF16), w_pb[0].astype(BF16), w_out[0].astype(BF16), norm_final)
    y_p, s_p, ck_p, cv_p = _layer(x_prompt, weights, hgrn_state=None, cache=None, t_valid=None)
    bs, ts, _ = x_sample.shape
    w = cache_k.shape[2]
    xs = jnp.pad(x_sample, ((0, 0), (0, SAMPLE_PAD - ts), (0, 0)))
    cache = (cache_k[0].reshape(bs, w, B_KVW), cache_v[0].reshape(bs, w, B_KVW))
    y_s, s_s, ck_s, cv_s = _layer(xs, weights, hgrn_state=state_hgrn[0], cache=cache, t_valid=ts)
    kvshape = lambda a: a.reshape(1, a.shape[0], a.shape[1], B_KVHEADS, B_HDIM)
    return (y_p, y_s[:, :ts], s_p[None], kvshape(ck_p), kvshape(cv_p),
            s_s[None], kvshape(ck_s), kvshape(cv_s))
```
